```python
import jax, jax.numpy as jnp
from jax import lax
import numpy as np

D_MODEL = 1024
BATCH = 8
SEQ = 4096
DEPTH = 1
DEC_BATCH = 128
DEC_SEQ = 4
PAST_LEN = 8192
PAGE_SIZE = 128

D_MIX = D_MODEL
D_LRU = D_MIX // 2
N_LRU_BLOCKS = 8
LRU_BLOCK = D_LRU // N_LRU_BLOCKS
LRU_C = 8.0
CONV_A = 4
D_ATT = D_MIX - D_LRU
N_HEADS = 8
HEAD_DIM = D_ATT // N_HEADS
ROT_DIM = HEAD_DIM // 4
ROPE_THETA = 500000.0
DILATED_PATTERNS = ((128, 1), (512, 4), (2048, 16))
MAX_WINDOW = 2048
ATTN_SCALE = HEAD_DIM ** -0.5
NEG_INF = -1e30
D_IN = 2 * D_LRU + 3 * D_ATT
D_FF = 3 * D_MODEL
CONV_F = 3
EPS = 1e-6

kernel_name = 'hymba_rglru_dilated_swa_convffn_step'


def rmsnorm(x, g):
    xf = x.astype(jnp.float32)
    y = xf * lax.rsqrt(jnp.mean(xf * xf, axis=-1, keepdims=True) + EPS)
    return (y * g.astype(jnp.float32)).astype(x.dtype)


def causal_dwconv(x_ext, w, b):
    width = w.shape[0]
    length = x_ext.shape[1] - width + 1
    out = b + x_ext[:, 0:length] * w[0]
    for j in range(1, width):
        out = out + x_ext[:, j:j + length] * w[j]
    return out


def partial_rope(x, pos):
    half = ROT_DIM // 2
    inv_freq = ROPE_THETA ** (-2.0 * jnp.arange(half, dtype=jnp.float32) / ROT_DIM)
    ang = pos[:, None] * inv_freq[None, :]
    cos = jnp.cos(ang)[None, :, None, :]
    sin = jnp.sin(ang)[None, :, None, :]
    xf = x.astype(jnp.float32)
    x1, x2, rest = xf[..., :half], xf[..., half:ROT_DIM], xf[..., ROT_DIM:]
    out = jnp.concatenate([x1 * cos - x2 * sin, x2 * cos + x1 * sin, rest], axis=-1)
    return out.astype(x.dtype)


def block_diag(x, w, b):
    n, l, _ = x.shape
    xb = x.reshape(n, l, N_LRU_BLOCKS, LRU_BLOCK)
    return jnp.einsum('nlgi,gij->nlgj', xb, w).reshape(n, l, D_LRU) + b


def rg_lru(x, h0, w_r, b_r, w_i, b_i, lam):
    r = jax.nn.sigmoid(block_diag(x, w_r, b_r).astype(jnp.float32))
    gi = jax.nn.sigmoid(block_diag(x, w_i, b_i).astype(jnp.float32))
    log_a = -LRU_C * r * jax.nn.softplus(-lam.astype(jnp.float32))
    a = jnp.exp(log_a)
    u = jnp.sqrt(-jnp.expm1(2.0 * log_a)) * (gi * x.astype(jnp.float32))
    u = u.at[:, 0].add(a[:, 0] * h0.astype(jnp.float32))

    def combine(c1, c2):
        a1, b1 = c1
        a2, b2 = c2
        return a1 * a2, a2 * b1 + b2

    _, h = lax.associative_scan(combine, (a, u), axis=1)
    return h


def dilated_attention_banded(q, k, v, window, dilation):
    n, t_len, h, dh = q.shape
    band = window // dilation
    span = band * dilation
    lp = -(-t_len // span) * span
    nb = lp // span
    pad = ((0, 0), (0, lp - t_len), (0, 0), (0, 0))

    def split(x):
        x = jnp.pad(x, pad).reshape(n, lp // dilation, dilation, h, dh).transpose(0, 2, 1, 3, 4)
        return x.reshape(n, dilation, nb, band, h, dh)

    def with_prev(x):
        prev = jnp.pad(x, ((0, 0), (0, 0), (1, 0), (0, 0), (0, 0), (0, 0)))[:, :, :-1]
        return jnp.concatenate([prev, x], axis=3)

    qs = split(q).astype(jnp.float32)
    kb = with_prev(split(k)).astype(jnp.float32)
    vb = with_prev(split(v)).astype(jnp.float32)
    s = jnp.einsum('nrbqhc,nrbkhc->nrbhqk', qs, kb) * ATTN_SCALE
    qi = jnp.arange(band)[:, None]
    kk = jnp.arange(2 * band)[None, :]
    rel = qi + band - kk
    blk = jnp.arange(nb)[:, None, None]
    ok = (rel >= 0) & (rel <= band) & ((blk > 0) | (kk >= band))
    s = jnp.where(ok[None, None, :, None], s, NEG_INF)
    lse = jax.nn.logsumexp(s, axis=-1)
    pr = jnp.exp(s - lse[..., None])
    o = jnp.einsum('nrbhqk,nrbkhc->nrbqhc', pr, vb)
    o = o.reshape(n, dilation, lp // dilation, h, dh).transpose(0, 2, 1, 3, 4).reshape(n, lp, h, dh)[:, :t_len]
    lse = lse.transpose(0, 1, 2, 4, 3).reshape(n, dilation, lp // dilation, h)
    lse = lse.transpose(0, 2, 1, 3).reshape(n, lp, h)[:, :t_len]
    return lse, o


def dilated_attention_gathered(q, kc, vc, n_hist, window, dilation):
    band = window // dilation
    s_len = q.shape[1]
    idx = n_hist + jnp.arange(s_len)[:, None] - dilation * jnp.arange(band + 1)[None, :]
    valid = idx >= 0
    idx = jnp.maximum(idx, 0)
    kg = kc[:, idx].astype(jnp.float32)
    vg = vc[:, idx].astype(jnp.float32)
    s = jnp.einsum('nshc,nskhc->nshk', q.astype(jnp.float32), kg) * ATTN_SCALE
    s = jnp.where(valid[None, :, None, :], s, NEG_INF)
    lse = jax.nn.logsumexp(s, axis=-1)
    pr = jnp.exp(s - lse[..., None])
    o = jnp.einsum('nshk,nskhc->nshc', pr, vg)
    return lse, o


def combine_dilations(parts):
    lse = jnp.stack([p[0] for p in parts], axis=0)
    o = jnp.stack([p[1] for p in parts], axis=0)
    w = jax.nn.softmax(lse, axis=0)
    return jnp.einsum('pnlh,pnlhc->nlhc', w, o)


def attend_prompt(q, k, v):
    return combine_dilations([dilated_attention_banded(q, k, v, w, d) for w, d in DILATED_PATTERNS])


def make_attend_sample(k_hist, v_hist):
    n_hist = k_hist.shape[1]

    def attend(q, k, v):
        kc = jnp.concatenate([k_hist.astype(k.dtype), k], axis=1)
        vc = jnp.concatenate([v_hist.astype(v.dtype), v], axis=1)
        return combine_dilations([dilated_attention_gathered(q, kc, vc, n_hist, w, d) for w, d in DILATED_PATTERNS])

    return attend


def decoder_layer(x, pos, conv_a_hist, lru_h0, ffn_hist, attend, p):
    n, l, _ = x.shape
    h = rmsnorm(x, p['g_mix'])
    z = h @ p['w_in']
    xa, ga, q, k, v = jnp.split(z, [D_LRU, 2 * D_LRU, 2 * D_LRU + D_ATT, 2 * D_LRU + 2 * D_ATT], axis=-1)
    xa_ext = jnp.concatenate([conv_a_hist.astype(x.dtype), xa], axis=1)
    xa_c = causal_dwconv(xa_ext, p['conv_a_w'], p['conv_a_b'])
    hs = rg_lru(xa_c, lru_h0, p['lru_w_r'], p['lru_b_r'], p['lru_w_i'], p['lru_b_i'], p['lru_lambda'])
    ya = hs.astype(x.dtype) * jax.nn.gelu(ga)
    q = partial_rope(q.reshape(n, l, N_HEADS, HEAD_DIM), pos)
    k = partial_rope(k.reshape(n, l, N_HEADS, HEAD_DIM), pos)
    v = v.reshape(n, l, N_HEADS, HEAD_DIM)
    yb = attend(q, k, v).astype(x.dtype).reshape(n, l, D_ATT)
    x = x + jnp.concatenate([ya, yb], axis=-1) @ p['w_out']
    h = rmsnorm(x, p['g_ffn'])
    uf = h @ p['w_ffn_gate']
    uf_ext = jnp.concatenate([ffn_hist.astype(x.dtype), uf], axis=1)
    uc = causal_dwconv(uf_ext, p['conv_f_w'], p['conv_f_b'])
    x = x + (jax.nn.gelu(uc) * (h @ p['w_ffn_up'])) @ p['w_ffn_down']
    new_state = (xa_ext[:, -(CONV_A - 1):], hs[:, -1].astype(x.dtype), k, v, uf_ext[:, -(CONV_F - 1):])
    return x, new_state


def setup_inputs(seed: int = 0) -> dict:
    key = jax.random.key(seed)
    ks = jax.random.split(key, 26)
    f32 = jnp.float32
    win = min(MAX_WINDOW, PAST_LEN)

    def normal(kk, shape, scale):
        return jax.random.normal(kk, shape, f32) * scale

    def gain(kk, shape):
        return 1.0 + 0.02 * jax.random.normal(kk, shape, f32)

    a_c = jax.random.uniform(ks[13], (DEPTH, D_LRU), f32, 0.9, 0.999)
    a_base = a_c ** (1.0 / LRU_C)
    lru_lambda = jnp.log(a_base) - jnp.log1p(-a_base)
    return {
        'x_prompt': normal(ks[0], (BATCH, SEQ, D_MODEL), 1.0),
        'x_sample': normal(ks[1], (DEC_BATCH, DEC_SEQ, D_MODEL), 1.0),
        'state_conv_a': normal(ks[2], (DEPTH, DEC_BATCH, CONV_A - 1, D_LRU), 1.0),
        'state_lru_h': normal(ks[3], (DEPTH, DEC_BATCH, D_LRU), 0.5),
        'cache_win_k': normal(ks[4], (DEPTH, DEC_BATCH, win, N_HEADS, HEAD_DIM), 1.0),
        'cache_win_v': normal(ks[5], (DEPTH, DEC_BATCH, win, N_HEADS, HEAD_DIM), 1.0),
        'state_conv_ffn': normal(ks[6], (DEPTH, DEC_BATCH, CONV_F - 1, D_FF), 1.0),
        'g_mix': gain(ks[7], (DEPTH, D_MODEL)),
        'w_in': normal(ks[8], (DEPTH, D_MODEL, D_IN), D_MODEL ** -0.5),
        'conv_a_w': normal(ks[9], (DEPTH, CONV_A, D_LRU), CONV_A ** -0.5),
        'conv_a_b': normal(ks[10], (DEPTH, D_LRU), 0.01),
        'lru_w_r': normal(ks[11], (DEPTH, N_LRU_BLOCKS, LRU_BLOCK, LRU_BLOCK), LRU_BLOCK ** -0.5),
        'lru_b_r': normal(ks[12], (DEPTH, D_LRU), 0.01),
        'lru_w_i': normal(ks[14], (DEPTH, N_LRU_BLOCKS, LRU_BLOCK, LRU_BLOCK), LRU_BLOCK ** -0.5),
        'lru_b_i': normal(ks[15], (DEPTH, D_LRU), 0.01),
        'lru_lambda': lru_lambda,
        'w_out': normal(ks[16], (DEPTH, D_MIX, D_MODEL), D_MIX ** -0.5),
        'g_ffn': gain(ks[17], (DEPTH, D_MODEL)),
        'w_ffn_gate': normal(ks[18], (DEPTH, D_MODEL, D_FF), D_MODEL ** -0.5),
        'conv_f_w': normal(ks[19], (DEPTH, CONV_F, D_FF), CONV_F ** -0.5),
        'conv_f_b': normal(ks[20], (DEPTH, D_FF), 0.01),
        'w_ffn_up': normal(ks[21], (DEPTH, D_MODEL, D_FF), D_MODEL ** -0.5),
        'w_ffn_down': normal(ks[22], (DEPTH, D_FF, D_MODEL), D_FF ** -0.5),
        'g_final': gain(ks[23], (D_MODEL,)),
    }


def reference(x_prompt, x_sample, state_conv_a, state_lru_h, cache_win_k, cache_win_v, state_conv_ffn,
              g_mix, w_in, conv_a_w, conv_a_b, lru_w_r, lru_b_r, lru_w_i, lru_b_i, lru_lambda, w_out,
              g_ffn, w_ffn_gate, conv_f_w, conv_f_b, w_ffn_up, w_ffn_down, g_final):
    n_p, seq_p, _ = x_prompt.shape
    n_s, seq_s, _ = x_sample.shape
    pos_p = jnp.arange(seq_p, dtype=jnp.float32)
    pos_s = PAST_LEN + jnp.arange(seq_s, dtype=jnp.float32)
    keep_p = min(MAX_WINDOW, seq_p)
    xp, xs = x_prompt, x_sample
    new_p, new_s = [], []
    for layer in range(DEPTH):
        p = {
            'g_mix': g_mix[layer], 'w_in': w_in[layer], 'conv_a_w': conv_a_w[layer], 'conv_a_b': conv_a_b[layer],
            'lru_w_r': lru_w_r[layer], 'lru_b_r': lru_b_r[layer], 'lru_w_i': lru_w_i[layer], 'lru_b_i': lru_b_i[layer],
            'lru_lambda': lru_lambda[layer], 'w_out': w_out[layer], 'g_ffn': g_ffn[layer],
            'w_ffn_gate': w_ffn_gate[layer], 'conv_f_w': conv_f_w[layer], 'conv_f_b': conv_f_b[layer],
            'w_ffn_up': w_ffn_up[layer], 'w_ffn_down': w_ffn_down[layer],
        }
        xp, st_p = decoder_layer(
            xp, pos_p,
            jnp.zeros((n_p, CONV_A - 1, D_LRU), xp.dtype),
            jnp.zeros((n_p, D_LRU), jnp.float32),
            jnp.zeros((n_p, CONV_F - 1, D_FF), xp.dtype),
            attend_prompt, p)
        xs, st_s = decoder_layer(
            xs, pos_s, state_conv_a[layer], state_lru_h[layer], state_conv_ffn[layer],
            make_attend_sample(cache_win_k[layer], cache_win_v[layer]), p)
        new_p.append((st_p[0], st_p[1], st_p[2][:, -keep_p:], st_p[3][:, -keep_p:], st_p[4]))
        new_s.append(st_s)
    y_prompt = rmsnorm(xp, g_final)
    y_sample = rmsnorm(xs, g_final)
    prompt_conv_a = jnp.stack([s[0] for s in new_p], axis=0)
    prompt_lru_h = jnp.stack([s[1] for s in new_p], axis=0)
    prompt_win_k = jnp.stack([s[2] for s in new_p], axis=0)
    prompt_win_v = jnp.stack([s[3] for s in new_p], axis=0)
    prompt_conv_ffn = jnp.stack([s[4] for s in new_p], axis=0)
    sample_conv_a = jnp.stack([s[0] for s in new_s], axis=0)
    sample_lru_h = jnp.stack([s[1] for s in new_s], axis=0)
    sample_win_k = jnp.stack([s[2] for s in new_s], axis=0)
    sample_win_v = jnp.stack([s[3] for s in new_s], axis=0)
    sample_conv_ffn = jnp.stack([s[4] for s in new_s], axis=0)
    return (y_prompt, y_sample, prompt_conv_a, prompt_lru_h, prompt_win_k, prompt_win_v, prompt_conv_ffn,
            sample_conv_a, sample_lru_h, sample_win_k, sample_win_v, sample_conv_ffn)
```

```python
import functools

import jax
import jax.numpy as jnp
from jax import lax
from jax.experimental import pallas as pl
from jax.experimental.pallas import tpu as pltpu

F32 = jnp.float32
BF16 = jnp.bfloat16

D_MODEL = 1024
D_LRU = 512
N_LRU_BLOCKS = 8
LRU_BLOCK = D_LRU // N_LRU_BLOCKS
LRU_C = 8.0
CONV_A = 4
D_ATT = 512
N_HEADS = 8
HEAD_DIM = 64
ROT_DIM = 16
ROT_HALF = ROT_DIM // 2
ROPE_THETA = 500000.0
DILATIONS = (1, 4, 16)
BAND = 128
MAX_WINDOW = 2048
PAST_LEN = 8192
ATTN_SCALE = HEAD_DIM ** -0.5
NEG_INF = -1e30
D_IN = 2 * D_LRU + 3 * D_ATT
D_FF = 3 * D_MODEL
CONV_F = 3
EPS = 1e-6

LANES = 128
SUBLANES = 8
HEADS_PER_BLOCK = LANES // HEAD_DIM
N_HEAD_BLOCKS = D_ATT // LANES
ATT_TILE = MAX_WINDOW
ROW_TILE = 512
FF_CHUNK = 512
VMEM_LIMIT = 56 * 1024 * 1024


def _rms(x, g):
    return x * lax.rsqrt(jnp.mean(x * x, axis=-1, keepdims=True) + EPS) * g


def _gelu(x):
    return 0.5 * x * (1.0 + jnp.tanh(0.7978845608028654 * (x + 0.044715 * (x * x * x))))


def _sigmoid(x):
    return 1.0 / (1.0 + jnp.exp(-x))


def _softplus(x):
    return jnp.maximum(x, 0.0) + jnp.log1p(jnp.exp(-jnp.abs(x)))


def _bdot(a, b):
    return jnp.dot(a.astype(BF16), b, preferred_element_type=F32)


def _shift_rows(x, tail, sh):
    rolled = pltpu.roll(x, sh, axis=0)
    rows = lax.broadcasted_iota(jnp.int32, tail.shape, 0)
    head = jnp.where(rows < sh, pltpu.roll(tail, sh, axis=0), rolled[0:SUBLANES])
    return jnp.concatenate([head, rolled[SUBLANES:]], axis=0)


def _rope(x, c, s1, s2):
    out = []
    for hb in range(N_HEAD_BLOCKS):
        blk = x[:, hb * LANES:(hb + 1) * LANES]
        out.append(blk * c + pltpu.roll(blk, ROT_HALF, axis=1) * s1
                   + pltpu.roll(blk, LANES - ROT_HALF, axis=1) * s2)
    return out


def _lru_coeffs(xa_c, wgate_ref, bgate_ref, lam_ref):
    gates = _bdot(xa_c, wgate_ref[...]) + bgate_ref[...]
    r = _sigmoid(gates[:, :D_LRU])
    gi = _sigmoid(gates[:, D_LRU:])
    log_a = -LRU_C * r * _softplus(-lam_ref[...])
    a = jnp.exp(log_a)
    u = jnp.sqrt(jnp.tanh(-log_a) * (1.0 + a * a)) * (gi * xa_c)
    return a, u


def _lru_scan(a, u, h0):
    n = a.shape[0]
    g = n // SUBLANES
    a3 = a.reshape(g, SUBLANES, D_LRU)
    u3 = u.reshape(g, SUBLANES, D_LRU)
    rows = lax.broadcasted_iota(jnp.int32, a3.shape, 1)
    sh = 1
    while sh < SUBLANES:
        keep = rows >= sh
        a_s = jnp.where(keep, pltpu.roll(a3, sh, axis=1), 1.0)
        u_s = jnp.where(keep, pltpu.roll(u3, sh, axis=1), 0.0)
        u3 = a3 * u_s + u3
        a3 = a3 * a_s
        sh *= 2
    carry = h0
    hs = []
    for i in range(g):
        hg = u3[i] + a3[i] * carry
        carry = hg[SUBLANES - 1:SUBLANES]
        hs.append(hg)
    return jnp.concatenate(hs, axis=0), carry


def _prompt_in_body(n_t, x_ref, g_ref, win_ref, cw_ref, cb_ref, wgate_ref, bgate_ref, lam_ref,
                    c_ref, s1_ref, s2_ref,
                    ya_ref, qa_ref, ka_ref, va_ref, kn_ref, vn_ref, ctail_ref, hlast_ref,
                    tail_scr, h_scr):
    t = pl.program_id(1)

    @pl.when(t == 0)
    def _():
        tail_scr[...] = jnp.zeros_like(tail_scr)
        h_scr[...] = jnp.zeros_like(h_scr)

    x = x_ref[...]
    z = _bdot(_rms(x, g_ref[...]), win_ref[...])
    xa = z[:, :D_LRU]
    ga = z[:, D_LRU:2 * D_LRU]
    q = z[:, 2 * D_LRU:2 * D_LRU + D_ATT]
    k = z[:, 2 * D_LRU + D_ATT:2 * D_LRU + 2 * D_ATT]
    v = z[:, 2 * D_LRU + 2 * D_ATT:]

    tail = tail_scr[...]
    xa_c = cb_ref[...] + xa * cw_ref[CONV_A - 1:CONV_A, :]
    for j in range(CONV_A - 1):
        xa_c = xa_c + _shift_rows(xa, tail, CONV_A - 1 - j) * cw_ref[j:j + 1, :]
    new_tail = xa[xa.shape[0] - SUBLANES:]
    tail_scr[...] = new_tail
    ctail_ref[...] = new_tail

    a, u = _lru_coeffs(xa_c, wgate_ref, bgate_ref, lam_ref)
    hs, carry = _lru_scan(a, u, h_scr[...])
    h_scr[...] = carry
    hlast_ref[...] = carry
    ya_ref[...] = (hs * _gelu(ga)).astype(BF16)

    c, s1, s2 = c_ref[...], s1_ref[...], s2_ref[...]
    q_blocks = _rope(q, c, s1, s2)
    k_blocks = _rope(k, c, s1, s2)
    for hb in range(N_HEAD_BLOCKS):
        qa_ref[hb] = q_blocks[hb] * ATTN_SCALE
        ka_ref[hb] = k_blocks[hb]
        va_ref[hb] = v[:, hb * LANES:(hb + 1) * LANES]

    @pl.when(t >= n_t - MAX_WINDOW // ROW_TILE)
    def _():
        kn_ref[...] = jnp.concatenate(k_blocks, axis=1)
        vn_ref[...] = v


def _prompt_in(x, g_mix, w_in, conv_w, conv_b, w_gate, b_gate, lam, tabs):
    b, t, _ = x.shape
    n_t = t // ROW_TILE
    keep_tiles = MAX_WINDOW // ROW_TILE
    whole = pl.BlockSpec(memory_space=pltpu.VMEM)
    tab_spec = pl.BlockSpec((ROW_TILE, LANES), lambda bi, ti: (ti, 0))
    att_spec = pl.BlockSpec((None, N_HEAD_BLOCKS, ROW_TILE, LANES), lambda bi, ti: (bi, 0, ti, 0))
    nat_spec = pl.BlockSpec((None, ROW_TILE, D_ATT),
                            lambda bi, ti: (bi, jnp.maximum(ti - (n_t - keep_tiles), 0), 0))
    att_shape = jax.ShapeDtypeStruct((b, N_HEAD_BLOCKS, t, LANES), F32)
    nat_shape = jax.ShapeDtypeStruct((b, MAX_WINDOW, D_ATT), F32)
    return pl.pallas_call(
        functools.partial(_prompt_in_body, n_t),
        grid=(b, n_t),
        in_specs=[pl.BlockSpec((None, ROW_TILE, D_MODEL), lambda bi, ti: (bi, ti, 0)),
                  whole, whole, whole, whole, whole, whole, whole, tab_spec, tab_spec, tab_spec],
        out_specs=[pl.BlockSpec((None, ROW_TILE, D_LRU), lambda bi, ti: (bi, ti, 0)),
                   att_spec, att_spec, att_spec, nat_spec, nat_spec,
                   pl.BlockSpec((None, SUBLANES, D_LRU), lambda bi, ti: (bi, 0, 0)),
                   pl.BlockSpec((None, 1, D_LRU), lambda bi, ti: (bi, 0, 0))],
        out_shape=[jax.ShapeDtypeStruct((b, t, D_LRU), BF16), att_shape, att_shape, att_shape,
                   nat_shape, nat_shape,
                   jax.ShapeDtypeStruct((b, SUBLANES, D_LRU), F32),
                   jax.ShapeDtypeStruct((b, 1, D_LRU), F32)],
        scratch_shapes=[pltpu.VMEM((SUBLANES, D_LRU), F32), pltpu.VMEM((1, D_LRU), F32)],
        compiler_params=pltpu.CompilerParams(dimension_semantics=("arbitrary", "arbitrary"),
                                             vmem_limit_bytes=VMEM_LIMIT),
        name="prompt_in",
    )(x, g_mix, w_in, conv_w, conv_b, w_gate, b_gate, lam, *tabs)


def _prompt_attn_body(q_ref, kc_ref, kp_ref, vc_ref, vp_ref, o_ref,
                      kk, vv, oacc0, oacc1, oacc2, lacc0, lacc1, lacc2):
    i = pl.program_id(2)
    kk[0:ATT_TILE] = kp_ref[...]
    kk[ATT_TILE:] = kc_ref[...]
    vv[0:ATT_TILE] = vp_ref[...]
    vv[ATT_TILE:] = vc_ref[...]

    qi = lax.broadcasted_iota(jnp.int32, (BAND, 2 * BAND), 0)
    ki = lax.broadcasted_iota(jnp.int32, (BAND, 2 * BAND), 1)
    band_ok = (ki >= qi) & (ki <= qi + BAND)
    lane = lax.broadcasted_iota(jnp.int32, (1, LANES), 1)
    head_lanes = [(lane >= hh * HEAD_DIM) & (lane < (hh + 1) * HEAD_DIM) for hh in range(HEADS_PER_BLOCK)]
    n_units = ATT_TILE // BAND

    for d, oacc, lacc in zip(DILATIONS, (oacc0, oacc1, oacc2), (lacc0, lacc1, lacc2)):
        def unit(un, carry, d=d, oacc=oacc, lacc=lacc):
            sb = un // d
            base_q = sb * (BAND * d) + un % d
            rows_q = pl.ds(base_q, BAND, stride=d)
            rows_k = pl.ds(ATT_TILE + base_q - BAND * d, 2 * BAND, stride=d)
            qu = q_ref[rows_q, :]
            ku = kk[rows_k, :].astype(BF16)
            vu = vv[rows_k, :]
            first_key = jnp.where((i == 0) & (sb == 0), BAND, 0)
            ok = band_ok & (ki >= first_key)
            out = jnp.zeros((BAND, LANES), F32)
            lse = jnp.zeros((BAND, LANES), F32)
            for sel in head_lanes:
                qh = jnp.where(sel, qu, 0.0).astype(BF16)
                vh = jnp.where(sel, vu, 0.0).astype(BF16)
                s = lax.dot_general(qh, ku, (((1,), (1,)), ((), ())), preferred_element_type=F32)
                s = jnp.where(ok, s, NEG_INF)
                m = jnp.max(s, axis=-1, keepdims=True)
                p = jnp.exp(s - m)
                l = jnp.sum(p, axis=-1, keepdims=True)
                out = out + jnp.dot(p.astype(BF16), vh, preferred_element_type=F32) / l
                lse = jnp.where(sel, m + jnp.log(l), lse)
            oacc[rows_q, :] = out
            lacc[rows_q, :] = lse
            return carry

        lax.fori_loop(0, n_units, unit, 0)

    chunk = 256

    def combine(ci, carry):
        rows = pl.ds(pl.multiple_of(ci * chunk, chunk), chunk)
        l0, l1, l2 = lacc0[rows, :], lacc1[rows, :], lacc2[rows, :]
        m = jnp.maximum(jnp.maximum(l0, l1), l2)
        w0, w1, w2 = jnp.exp(l0 - m), jnp.exp(l1 - m), jnp.exp(l2 - m)
        num = w0 * oacc0[rows, :] + w1 * oacc1[rows, :] + w2 * oacc2[rows, :]
        o_ref[rows, :] = (num / (w0 + w1 + w2)).astype(BF16)
        return carry

    lax.fori_loop(0, ATT_TILE // chunk, combine, 0)


def _prompt_attn(qa, ka, va):
    b, _, t, _ = qa.shape
    n_i = t // ATT_TILE
    cur = pl.BlockSpec((None, None, ATT_TILE, LANES), lambda bi, hb, ii: (bi, hb, ii, 0))
    prev = pl.BlockSpec((None, None, ATT_TILE, LANES),
                        lambda bi, hb, ii: (bi, hb, jnp.maximum(ii - 1, 0), 0))
    acc = pltpu.VMEM((ATT_TILE, LANES), F32)
    return pl.pallas_call(
        _prompt_attn_body,
        grid=(b, N_HEAD_BLOCKS, n_i),
        in_specs=[cur, cur, prev, cur, prev],
        out_specs=pl.BlockSpec((None, ATT_TILE, LANES), lambda bi, hb, ii: (bi, ii, hb)),
        out_shape=jax.ShapeDtypeStruct((b, t, D_ATT), BF16),
        scratch_shapes=[pltpu.VMEM((2 * ATT_TILE, LANES), F32), pltpu.VMEM((2 * ATT_TILE, LANES), F32),
                        acc, acc, acc, acc, acc, acc],
        compiler_params=pltpu.CompilerParams(dimension_semantics=("arbitrary",) * 3,
                                             vmem_limit_bytes=VMEM_LIMIT),
        name="prompt_attn",
    )(qa, ka, ka, va, va)


def _out_ffn(x, ya, yb, wo_ref, gf_ref, wg_ref, wu_ref, wd_ref, cw_ref, cb_ref, gfin_ref, act_scr,
             shifted):
    x1 = (x + jnp.dot(ya, wo_ref[0:D_LRU, :], preferred_element_type=F32)
          + jnp.dot(yb, wo_ref[D_LRU:, :], preferred_element_type=F32))
    h = _rms(x1, gf_ref[...]).astype(BF16)
    for ci in range(D_FF // FF_CHUNK):
        cols = slice(ci * FF_CHUNK, (ci + 1) * FF_CHUNK)
        uf = jnp.dot(h, wg_ref[:, cols], preferred_element_type=F32)
        back1, back2 = shifted(uf, cols)
        uc = (cb_ref[:, cols] + back2 * cw_ref[0:1, cols] + back1 * cw_ref[1:2, cols]
              + uf * cw_ref[2:3, cols])
        up = jnp.dot(h, wu_ref[:, cols], preferred_element_type=F32)
        act_scr[:, cols] = (_gelu(uc) * up).astype(BF16)
    x2 = x1 + jnp.dot(act_scr[...], wd_ref[...], preferred_element_type=F32)
    return _rms(x2, gfin_ref[...])


def _prompt_out_body(x_ref, ya_ref, yb_ref, wo_ref, gf_ref, wg_ref, wu_ref, wd_ref, cw_ref, cb_ref,
                     gfin_ref, y_ref, ftail_ref, tail_scr, act_scr):
    t = pl.program_id(1)

    @pl.when(t == 0)
    def _():
        tail_scr[...] = jnp.zeros_like(tail_scr)

    def shifted(uf, cols):
        tail = tail_scr[:, cols]
        tail_scr[:, cols] = uf[uf.shape[0] - SUBLANES:]
        return _shift_rows(uf, tail, 1), _shift_rows(uf, tail, 2)

    y_ref[...] = _out_ffn(x_ref[...], ya_ref[...], yb_ref[...], wo_ref, gf_ref, wg_ref, wu_ref, wd_ref,
                          cw_ref, cb_ref, gfin_ref, act_scr, shifted)
    ftail_ref[...] = tail_scr[...]


def _prompt_out(x, ya, yb, w_out, g_ffn, w_gate, w_up, w_down, conv_w, conv_b, g_final):
    b, t, _ = x.shape
    whole = pl.BlockSpec(memory_space=pltpu.VMEM)

    def rows(width):
        return pl.BlockSpec((None, ROW_TILE, width), lambda bi, ti: (bi, ti, 0))

    return pl.pallas_call(
        _prompt_out_body,
        grid=(b, t // ROW_TILE),
        in_specs=[rows(D_MODEL), rows(D_LRU), rows(D_ATT)] + [whole] * 8,
        out_specs=[rows(D_MODEL), pl.BlockSpec((None, SUBLANES, D_FF), lambda bi, ti: (bi, 0, 0))],
        out_shape=[jax.ShapeDtypeStruct((b, t, D_MODEL), F32),
                   jax.ShapeDtypeStruct((b, SUBLANES, D_FF), F32)],
        scratch_shapes=[pltpu.VMEM((SUBLANES, D_FF), F32), pltpu.VMEM((ROW_TILE, D_FF), BF16)],
        compiler_params=pltpu.CompilerParams(dimension_semantics=("arbitrary", "arbitrary"),
                                             vmem_limit_bytes=VMEM_LIMIT),
        name="prompt_out",
    )(x, ya, yb, w_out, g_ffn, w_gate, w_up, w_down, conv_w, conv_b, g_final)


def _sample_in_body(n_b, n_s, x_ref, hist_ref, h0_ref, g_ref, win_ref, cw_ref, cb_ref, wgate_ref,
                    bgate_ref, lam_ref, c_ref, s1_ref, s2_ref,
                    ya_ref, q_ref, k_ref, v_ref, cst_ref, hlast_ref):
    z = _bdot(_rms(x_ref[...], g_ref[...]), win_ref[...])
    xa = z[:, :D_LRU]
    ga = z[:, D_LRU:2 * D_LRU]
    q = z[:, 2 * D_LRU:2 * D_LRU + D_ATT]
    k = z[:, 2 * D_LRU + D_ATT:2 * D_LRU + 2 * D_ATT]
    v = z[:, 2 * D_LRU + 2 * D_ATT:]

    ext = jnp.concatenate([hist_ref[...], xa], axis=0)
    xa_c = cb_ref[...] + ext[0:n_s * n_b] * cw_ref[0:1, :]
    for j in range(1, CONV_A):
        xa_c = xa_c + ext[j * n_b:(j + n_s) * n_b] * cw_ref[j:j + 1, :]
    cst_ref[...] = ext[n_s * n_b:]

    a, u = _lru_coeffs(xa_c, wgate_ref, bgate_ref, lam_ref)
    h = h0_ref[...]
    hs = []
    for s in range(n_s):
        h = a[s * n_b:(s + 1) * n_b] * h + u[s * n_b:(s + 1) * n_b]
        hs.append(h)
    hlast_ref[...] = h
    ya_ref[...] = (jnp.concatenate(hs, axis=0) * _gelu(ga)).astype(BF16)

    c, s1, s2 = c_ref[...], s1_ref[...], s2_ref[...]
    q_ref[...] = jnp.concatenate(_rope(q, c, s1, s2), axis=1) * ATTN_SCALE
    k_ref[...] = jnp.concatenate(_rope(k, c, s1, s2), axis=1)
    v_ref[...] = v


def _sample_in(x, hist, h0, g_mix, w_in, conv_w, conv_b, w_gate, b_gate, lam, tabs, n_b, n_s):
    rows = n_b * n_s
    att = jax.ShapeDtypeStruct((rows, D_ATT), F32)
    return pl.pallas_call(
        functools.partial(_sample_in_body, n_b, n_s),
        out_shape=[jax.ShapeDtypeStruct((rows, D_LRU), BF16), att, att, att,
                   jax.ShapeDtypeStruct(((CONV_A - 1) * n_b, D_LRU), F32),
                   jax.ShapeDtypeStruct((n_b, D_LRU), F32)],
        compiler_params=pltpu.CompilerParams(vmem_limit_bytes=VMEM_LIMIT),
        name="sample_in",
    )(x, hist, h0, g_mix, w_in, conv_w, conv_b, w_gate, b_gate, lam, *tabs)


SAMPLE_BB = 2
TAIL_ROWS = 512
GROUP = 16


def _sample_attn_body(n_s, q_ref, kn_ref, vn_ref, k3_ref, v3_ref, k2_ref, v2_ref, e_ref, et_ref, o_ref):
    e = e_ref[...]
    et = et_ref[...]
    n3 = k3_ref.shape[1]
    n2 = TAIL_ROWS // SUBLANES
    n1 = BAND // SUBLANES
    sub3 = lax.broadcasted_iota(jnp.int32, (n3, SUBLANES, LANES), 1)
    row1 = lax.broadcasted_iota(jnp.int32, (BAND, LANES), 0)
    new_r = lax.broadcasted_iota(jnp.int32, (n_s, LANES), 0)

    def scores(prod):
        return jnp.dot(prod.astype(BF16), e, preferred_element_type=F32)

    def spread(p):
        return jnp.dot(p.astype(BF16), et, preferred_element_type=F32)

    for bi in range(SAMPLE_BB):
        q = q_ref[bi]
        k_new, v_new = kn_ref[bi], vn_ref[bi]
        zeros = jnp.zeros((SUBLANES - n_s, D_ATT), F32)
        q3 = jnp.concatenate([q, zeros], axis=0)
        q2 = jnp.concatenate([q, q], axis=0)
        s3 = scores((k3_ref[bi] * q3[None]).reshape(n3 * SUBLANES, D_ATT)).reshape(n3, SUBLANES, LANES)
        s3 = jnp.where(sub3 < n_s, s3, NEG_INF)
        s2 = scores((k2_ref[bi].reshape(n2, SUBLANES, D_ATT) * q2[None]).reshape(TAIL_ROWS, D_ATT))
        s2 = s2.reshape(n2, SUBLANES, LANES)
        k1 = k2_ref[bi, TAIL_ROWS - BAND:, :]
        s1 = [jnp.where(row1 >= s, scores(k1 * q[s:s + 1]), NEG_INF) for s in range(n_s)]
        sn = [scores(k_new * q[s:s + 1]) for s in range(n_s)]
        sn_m = [jnp.where(new_r <= s, sn[s], NEG_INF) for s in range(n_s)]

        m3 = jnp.max(s3, axis=0)
        m2 = jnp.max(s2, axis=0)
        m = jnp.maximum(m3[0:n_s], jnp.maximum(m2[0:n_s], m2[n_s:]))
        m1 = jnp.concatenate([jnp.max(x, axis=0, keepdims=True) for x in s1], axis=0)
        mn = jnp.concatenate([jnp.max(x, axis=0, keepdims=True) for x in sn_m], axis=0)
        m = jnp.maximum(m, jnp.maximum(m1, mn))
        m8 = jnp.concatenate([m, m], axis=0)

        p3 = jnp.exp(s3 - m8[None])
        p2 = jnp.exp(s2 - m8[None])
        p1 = [jnp.exp(s1[s] - m[s:s + 1]) for s in range(n_s)]
        mult = [jnp.where(new_r < s, 1.0, jnp.where(new_r == s, 3.0, 0.0)) for s in range(n_s)]
        pn = [mult[s] * jnp.exp(sn_m[s] - m[s:s + 1]) for s in range(n_s)]

        l3 = jnp.sum(p3, axis=0)
        l2 = jnp.sum(p2, axis=0)
        l = l3[0:n_s] + l2[0:n_s] + l2[n_s:]
        l = l + jnp.concatenate([jnp.sum(x, axis=0, keepdims=True) for x in p1], axis=0)
        l = l + jnp.concatenate([jnp.sum(x, axis=0, keepdims=True) for x in pn], axis=0)
        inv = 1.0 / l
        inv8 = jnp.concatenate([inv, inv], axis=0)

        w3 = spread((p3 * inv8[None]).reshape(n3 * SUBLANES, LANES)).reshape(n3, SUBLANES, D_ATT)
        acc = jnp.sum(w3 * v3_ref[bi], axis=0)
        w2 = spread((p2 * inv8[None]).reshape(TAIL_ROWS, LANES)).reshape(n2, SUBLANES, D_ATT)
        acc2 = jnp.sum(w2 * v2_ref[bi].reshape(n2, SUBLANES, D_ATT), axis=0)
        out = acc[0:n_s] + acc2[0:n_s] + acc2[n_s:]
        v1 = v2_ref[bi, TAIL_ROWS - BAND:, :]
        rest = []
        for s in range(n_s):
            o1 = jnp.sum(spread(p1[s] * inv[s:s + 1]) * v1, axis=0, keepdims=True)
            on = jnp.sum(spread(pn[s] * inv[s:s + 1]) * v_new, axis=0, keepdims=True)
            rest.append(o1 + on)
        o_ref[bi] = out + jnp.concatenate(rest, axis=0)


def _sample_attn(q, k_new, v_new, cache_k, cache_v, e, et):
    n_b, n_s, _ = q.shape
    win = cache_k.shape[1]
    k3 = cache_k.reshape(n_b, win // GROUP, GROUP, D_ATT)
    v3 = cache_v.reshape(n_b, win // GROUP, GROUP, D_ATT)
    small = pl.BlockSpec((SAMPLE_BB, n_s, D_ATT), lambda bi: (bi, 0, 0))
    grp = pl.BlockSpec((SAMPLE_BB, win // GROUP, SUBLANES, D_ATT), lambda bi: (bi, 0, 0, 0))
    tail = pl.BlockSpec((SAMPLE_BB, TAIL_ROWS, D_ATT), lambda bi: (bi, win // TAIL_ROWS - 1, 0))
    whole = pl.BlockSpec(memory_space=pltpu.VMEM)
    return pl.pallas_call(
        functools.partial(_sample_attn_body, n_s),
        grid=(n_b // SAMPLE_BB,),
        in_specs=[small, small, small, grp, grp, tail, tail, whole, whole],
        out_specs=small,
        out_shape=jax.ShapeDtypeStruct((n_b, n_s, D_ATT), F32),
        compiler_params=pltpu.CompilerParams(dimension_semantics=("arbitrary",),
                                             vmem_limit_bytes=VMEM_LIMIT),
        name="sample_attn",
    )(q, k_new, v_new, k3, v3, cache_k, cache_v, e, et)


def _sample_out_body(n_b, x_ref, ya_ref, yb_ref, hist_ref, wo_ref, gf_ref, wg_ref, wu_ref, wd_ref,
                     cw_ref, cb_ref, gfin_ref, y_ref, fst_ref, act_scr):
    n_rows = x_ref.shape[0]

    def shifted(uf, cols):
        ext = jnp.concatenate([hist_ref[:, cols], uf], axis=0)
        fst_ref[:, cols] = ext[n_rows:]
        return ext[n_b:n_b + n_rows], ext[0:n_rows]

    y_ref[...] = _out_ffn(x_ref[...], ya_ref[...], yb_ref[...].astype(BF16), wo_ref, gf_ref, wg_ref,
                          wu_ref, wd_ref, cw_ref, cb_ref, gfin_ref, act_scr, shifted)


def _sample_out(x, ya, yb, hist, w_out, g_ffn, w_gate, w_up, w_down, conv_w, conv_b, g_final, n_b):
    rows = x.shape[0]
    return pl.pallas_call(
        functools.partial(_sample_out_body, n_b),
        out_shape=[jax.ShapeDtypeStruct((rows, D_MODEL), F32),
                   jax.ShapeDtypeStruct(((CONV_F - 1) * n_b, D_FF), F32)],
        scratch_shapes=[pltpu.VMEM((rows, D_FF), BF16)],
        compiler_params=pltpu.CompilerParams(vmem_limit_bytes=VMEM_LIMIT),
        name="sample_out",
    )(x, ya, yb, hist, w_out, g_ffn, w_gate, w_up, w_down, conv_w, conv_b, g_final)


def _rope_tables(pos):
    n = pos.shape[0]
    inv_freq = ROPE_THETA ** (-2.0 * jnp.arange(ROT_HALF, dtype=F32) / ROT_DIM)
    ang = pos[:, None] * inv_freq[None, :]
    cos, sin = jnp.cos(ang), jnp.sin(ang)
    rest = HEAD_DIM - ROT_DIM
    zeros_h = jnp.zeros((n, ROT_HALF), F32)
    c = jnp.concatenate([cos, cos, jnp.ones((n, rest), F32)], axis=1)
    s1 = jnp.concatenate([zeros_h, sin, jnp.zeros((n, rest), F32)], axis=1)
    s2 = jnp.concatenate([-sin, zeros_h, jnp.zeros((n, rest), F32)], axis=1)
    return tuple(jnp.tile(x, (1, HEADS_PER_BLOCK)) for x in (c, s1, s2))


def _block_diag(w):
    nb, n, _ = w.shape
    eye = jnp.eye(nb, dtype=w.dtype)
    return (eye[:, None, :, None] * w[:, :, None, :]).reshape(nb * n, nb * n)


def kernel(x_prompt, x_sample, state_conv_a, state_lru_h, cache_win_k, cache_win_v, state_conv_ffn,
           g_mix, w_in, conv_a_w, conv_a_b, lru_w_r, lru_b_r, lru_w_i, lru_b_i, lru_lambda, w_out,
           g_ffn, w_ffn_gate, conv_f_w, conv_f_b, w_ffn_up, w_ffn_down, g_final):
    n_p, seq_p, _ = x_prompt.shape
    n_s, seq_s, _ = x_sample.shape
    depth = w_in.shape[0]
    win = cache_win_k.shape[2]
    assert depth == 1
    assert seq_p % ATT_TILE == 0 and win == MAX_WINDOW and 2 * seq_s == SUBLANES
    assert n_s % SAMPLE_BB == 0

    tabs_p = _rope_tables(jnp.arange(seq_p, dtype=F32))
    tabs_s = tuple(jnp.repeat(x, n_s, axis=0)
                   for x in _rope_tables(PAST_LEN + jnp.arange(seq_s, dtype=F32)))
    head_of_lane = jnp.arange(D_ATT) // HEAD_DIM
    e = (head_of_lane[:, None] == jnp.arange(LANES)[None, :]).astype(BF16)
    et = e.T
    g_fin = g_final.reshape(1, D_MODEL)

    xp = x_prompt
    xs = x_sample.transpose(1, 0, 2).reshape(seq_s * n_s, D_MODEL)
    outs = [[] for _ in range(10)]
    for layer in range(depth):
        w_in_b = w_in[layer].astype(BF16)
        w_gate = jnp.concatenate([_block_diag(lru_w_r[layer]), _block_diag(lru_w_i[layer])],
                                 axis=1).astype(BF16)
        b_gate = jnp.concatenate([lru_b_r[layer], lru_b_i[layer]]).reshape(1, 2 * D_LRU)
        lam = lru_lambda[layer].reshape(1, D_LRU)
        g_mix_l = g_mix[layer].reshape(1, D_MODEL)
        g_ffn_l = g_ffn[layer].reshape(1, D_MODEL)
        conv_a_b_l = conv_a_b[layer].reshape(1, D_LRU)
        conv_f_b_l = conv_f_b[layer].reshape(1, D_FF)
        w_out_b = w_out[layer].astype(BF16)
        w_fg = w_ffn_gate[layer].astype(BF16)
        w_fu = w_ffn_up[layer].astype(BF16)
        w_fd = w_ffn_down[layer].astype(BF16)

        ya, qa, ka, va, kn, vn, ctail, hlast = _prompt_in(
            xp, g_mix_l, w_in_b, conv_a_w[layer], conv_a_b_l, w_gate, b_gate, lam, tabs_p)
        yb = _prompt_attn(qa, ka, va)
        yp, ftail = _prompt_out(xp, ya, yb, w_out_b, g_ffn_l, w_fg, w_fu, w_fd, conv_f_w[layer],
                                conv_f_b_l, g_fin)
        outs[0].append(ctail[:, SUBLANES - (CONV_A - 1):])
        outs[1].append(hlast[:, 0])
        outs[2].append(kn.reshape(n_p, MAX_WINDOW, N_HEADS, HEAD_DIM))
        outs[3].append(vn.reshape(n_p, MAX_WINDOW, N_HEADS, HEAD_DIM))
        outs[4].append(ftail[:, SUBLANES - (CONV_F - 1):])

        hist_a = state_conv_a[layer].transpose(1, 0, 2).reshape((CONV_A - 1) * n_s, D_LRU)
        hist_f = state_conv_ffn[layer].transpose(1, 0, 2).reshape((CONV_F - 1) * n_s, D_FF)
        ya_s, q_s, k_s, v_s, cst_s, hlast_s = _sample_in(
            xs, hist_a, state_lru_h[layer], g_mix_l, w_in_b, conv_a_w[layer], conv_a_b_l, w_gate,
            b_gate, lam, tabs_s, n_s, seq_s)

        def batch_major(x):
            return x.reshape(seq_s, n_s, -1).transpose(1, 0, 2)

        k_bm, v_bm = batch_major(k_s), batch_major(v_s)
        yb_s = _sample_attn(batch_major(q_s), k_bm, v_bm,
                            cache_win_k[layer].reshape(n_s, win, D_ATT),
                            cache_win_v[layer].reshape(n_s, win, D_ATT), e, et)
        yb_s = yb_s.transpose(1, 0, 2).reshape(seq_s * n_s, D_ATT)
        ys, fst_s = _sample_out(xs, ya_s, yb_s, hist_f, w_out_b, g_ffn_l, w_fg, w_fu, w_fd,
                                conv_f_w[layer], conv_f_b_l, g_fin, n_s)
        outs[5].append(cst_s.reshape(CONV_A - 1, n_s, D_LRU).transpose(1, 0, 2))
        outs[6].append(hlast_s)
        outs[7].append(k_bm.reshape(n_s, seq_s, N_HEADS, HEAD_DIM))
        outs[8].append(v_bm.reshape(n_s, seq_s, N_HEADS, HEAD_DIM))
        outs[9].append(fst_s.reshape(CONV_F - 1, n_s, D_FF).transpose(1, 0, 2))
        xp, xs = yp, ys

    y_sample = xs.reshape(seq_s, n_s, D_MODEL).transpose(1, 0, 2)
    return (xp, y_sample) + tuple(jnp.stack(o, axis=0) for o in outs)
```

```python
import functools

import jax
import jax.numpy as jnp
from jax import lax
from jax.experimental import pallas as pl
from jax.experimental.pallas import tpu as pltpu

F32 = jnp.float32
BF16 = jnp.bfloat16

D_MODEL = 1024
D_LRU = 512
N_LRU_BLOCKS = 8
LRU_BLOCK = D_LRU // N_LRU_BLOCKS
LRU_C = 8.0
CONV_A = 4
D_ATT = 512
N_HEADS = 8
HEAD_DIM = 64
ROT_DIM = 16
ROT_HALF = ROT_DIM // 2
ROPE_THETA = 500000.0
DILATIONS = (1, 4, 16)
BAND = 128
MAX_WINDOW = 2048
PAST_LEN = 8192
ATTN_SCALE = HEAD_DIM ** -0.5
NEG_INF = -1e30
D_IN = 2 * D_LRU + 3 * D_ATT
D_FF = 3 * D_MODEL
CONV_F = 3
EPS = 1e-6

LANES = 128
SUBLANES = 8
HEADS_PER_BLOCK = LANES // HEAD_DIM
N_HEAD_BLOCKS = D_ATT // LANES
ATT_TILE = MAX_WINDOW
ROW_TILE = 512
FF_CHUNK = 512
VMEM_LIMIT = 56 * 1024 * 1024


def _rms(x, g):
    return x * lax.rsqrt(jnp.mean(x * x, axis=-1, keepdims=True) + EPS) * g


def _gelu(x):
    return 0.5 * x * (1.0 + jnp.tanh(0.7978845608028654 * (x + 0.044715 * (x * x * x))))


def _sigmoid(x):
    return 1.0 / (1.0 + jnp.exp(-x))


def _softplus(x):
    return jnp.maximum(x, 0.0) + jnp.log1p(jnp.exp(-jnp.abs(x)))


def _bdot(a, b):
    return jnp.dot(a.astype(BF16), b, preferred_element_type=F32)


def _shift_rows(x, tail, sh):
    rolled = pltpu.roll(x, sh, axis=0)
    rows = lax.broadcasted_iota(jnp.int32, tail.shape, 0)
    head = jnp.where(rows < sh, pltpu.roll(tail, sh, axis=0), rolled[0:SUBLANES])
    return jnp.concatenate([head, rolled[SUBLANES:]], axis=0)


def _rope(x, c, s1, s2):
    out = []
    for hb in range(N_HEAD_BLOCKS):
        blk = x[:, hb * LANES:(hb + 1) * LANES]
        out.append(blk * c + pltpu.roll(blk, ROT_HALF, axis=1) * s1
                   + pltpu.roll(blk, LANES - ROT_HALF, axis=1) * s2)
    return out


def _lru_coeffs(xa_c, wgate_ref, bgate_ref, lam_ref):
    gates = _bdot(xa_c, wgate_ref[...]) + bgate_ref[...]
    r = _sigmoid(gates[:, :D_LRU])
    gi = _sigmoid(gates[:, D_LRU:])
    log_a = -LRU_C * r * _softplus(-lam_ref[...])
    a = jnp.exp(log_a)
    u = jnp.sqrt(jnp.tanh(-log_a) * (1.0 + a * a)) * (gi * xa_c)
    return a, u


def _lru_scan(a, u, h0):
    n = a.shape[0]
    g = n // SUBLANES
    a3 = a.reshape(g, SUBLANES, D_LRU)
    u3 = u.reshape(g, SUBLANES, D_LRU)
    rows = lax.broadcasted_iota(jnp.int32, a3.shape, 1)
    sh = 1
    while sh < SUBLANES:
        keep = rows >= sh
        a_s = jnp.where(keep, pltpu.roll(a3, sh, axis=1), 1.0)
        u_s = jnp.where(keep, pltpu.roll(u3, sh, axis=1), 0.0)
        u3 = a3 * u_s + u3
        a3 = a3 * a_s
        sh *= 2
    carry = h0
    hs = []
    for i in range(g):
        hg = u3[i] + a3[i] * carry
        carry = hg[SUBLANES - 1:SUBLANES]
        hs.append(hg)
    return jnp.concatenate(hs, axis=0), carry


def _prompt_in_body(n_t, x_ref, g_ref, win_ref, cw_ref, cb_ref, wgate_ref, bgate_ref, lam_ref,
                    c_ref, s1_ref, s2_ref,
                    ya_ref, qa_ref, ka_ref, va_ref, kn_ref, vn_ref, ctail_ref, hlast_ref,
                    tail_scr, h_scr):
    t = pl.program_id(1)

    @pl.when(t == 0)
    def _():
        tail_scr[...] = jnp.zeros_like(tail_scr)
        h_scr[...] = jnp.zeros_like(h_scr)

    x = x_ref[...]
    z = _bdot(_rms(x, g_ref[...]), win_ref[...])
    xa = z[:, :D_LRU]
    ga = z[:, D_LRU:2 * D_LRU]
    q = z[:, 2 * D_LRU:2 * D_LRU + D_ATT]
    k = z[:, 2 * D_LRU + D_ATT:2 * D_LRU + 2 * D_ATT]
    v = z[:, 2 * D_LRU + 2 * D_ATT:]

    tail = tail_scr[...]
    xa_c = cb_ref[...] + xa * cw_ref[CONV_A - 1:CONV_A, :]
    for j in range(CONV_A - 1):
        xa_c = xa_c + _shift_rows(xa, tail, CONV_A - 1 - j) * cw_ref[j:j + 1, :]
    new_tail = xa[xa.shape[0] - SUBLANES:]
    tail_scr[...] = new_tail
    ctail_ref[...] = new_tail

    a, u = _lru_coeffs(xa_c, wgate_ref, bgate_ref, lam_ref)
    hs, carry = _lru_scan(a, u, h_scr[...])
    h_scr[...] = carry
    hlast_ref[...] = carry
    ya_ref[...] = (hs * _gelu(ga)).astype(BF16)

    c, s1, s2 = c_ref[...], s1_ref[...], s2_ref[...]
    q_blocks = _rope(q, c, s1, s2)
    k_blocks = _rope(k, c, s1, s2)
    for hb in range(N_HEAD_BLOCKS):
        qa_ref[hb] = q_blocks[hb] * ATTN_SCALE
        ka_ref[hb] = k_blocks[hb]
        va_ref[hb] = v[:, hb * LANES:(hb + 1) * LANES]

    @pl.when(t >= n_t - MAX_WINDOW // ROW_TILE)
    def _():
        kn_ref[...] = jnp.concatenate(k_blocks, axis=1)
        vn_ref[...] = v


def _prompt_in(x, g_mix, w_in, conv_w, conv_b, w_gate, b_gate, lam, tabs):
    b, t, _ = x.shape
    n_t = t // ROW_TILE
    keep_tiles = MAX_WINDOW // ROW_TILE
    whole = pl.BlockSpec(memory_space=pltpu.VMEM)
    tab_spec = pl.BlockSpec((ROW_TILE, LANES), lambda bi, ti: (ti, 0))
    att_spec = pl.BlockSpec((None, N_HEAD_BLOCKS, ROW_TILE, LANES), lambda bi, ti: (bi, 0, ti, 0))
    nat_spec = pl.BlockSpec((None, ROW_TILE, D_ATT),
                            lambda bi, ti: (bi, jnp.maximum(ti - (n_t - keep_tiles), 0), 0))
    att_shape = jax.ShapeDtypeStruct((b, N_HEAD_BLOCKS, t, LANES), F32)
    nat_shape = jax.ShapeDtypeStruct((b, MAX_WINDOW, D_ATT), F32)
    return pl.pallas_call(
        functools.partial(_prompt_in_body, n_t),
        grid=(b, n_t),
        in_specs=[pl.BlockSpec((None, ROW_TILE, D_MODEL), lambda bi, ti: (bi, ti, 0)),
                  whole, whole, whole, whole, whole, whole, whole, tab_spec, tab_spec, tab_spec],
        out_specs=[pl.BlockSpec((None, ROW_TILE, D_LRU), lambda bi, ti: (bi, ti, 0)),
                   att_spec, att_spec, att_spec, nat_spec, nat_spec,
                   pl.BlockSpec((None, SUBLANES, D_LRU), lambda bi, ti: (bi, 0, 0)),
                   pl.BlockSpec((None, 1, D_LRU), lambda bi, ti: (bi, 0, 0))],
        out_shape=[jax.ShapeDtypeStruct((b, t, D_LRU), BF16), att_shape, att_shape, att_shape,
                   nat_shape, nat_shape,
                   jax.ShapeDtypeStruct((b, SUBLANES, D_LRU), F32),
                   jax.ShapeDtypeStruct((b, 1, D_LRU), F32)],
        scratch_shapes=[pltpu.VMEM((SUBLANES, D_LRU), F32), pltpu.VMEM((1, D_LRU), F32)],
        compiler_params=pltpu.CompilerParams(dimension_semantics=("arbitrary", "arbitrary"),
                                             vmem_limit_bytes=VMEM_LIMIT),
        name="prompt_in",
    )(x, g_mix, w_in, conv_w, conv_b, w_gate, b_gate, lam, *tabs)


def _prompt_attn_body(q_ref, kc_ref, kp_ref, vc_ref, vp_ref, o_ref,
                      kk, vv, oacc0, oacc1, oacc2, lacc0, lacc1, lacc2):
    i = pl.program_id(2)
    kk[0:ATT_TILE] = kp_ref[...]
    kk[ATT_TILE:] = kc_ref[...]
    vv[0:ATT_TILE] = vp_ref[...]
    vv[ATT_TILE:] = vc_ref[...]

    qi = lax.broadcasted_iota(jnp.int32, (BAND, 2 * BAND), 0)
    ki = lax.broadcasted_iota(jnp.int32, (BAND, 2 * BAND), 1)
    band_ok = (ki >= qi) & (ki <= qi + BAND)
    lane = lax.broadcasted_iota(jnp.int32, (1, LANES), 1)
    head_lanes = [(lane >= hh * HEAD_DIM) & (lane < (hh + 1) * HEAD_DIM) for hh in range(HEADS_PER_BLOCK)]
    n_units = ATT_TILE // BAND

    for d, oacc, lacc in zip(DILATIONS, (oacc0, oacc1, oacc2), (lacc0, lacc1, lacc2)):
        def unit(un, carry, d=d, oacc=oacc, lacc=lacc):
            sb = un // d
            base_q = sb * (BAND * d) + un % d
            rows_q = pl.ds(base_q, BAND, stride=d)
            rows_k = pl.ds(ATT_TILE + base_q - BAND * d, 2 * BAND, stride=d)
            qu = q_ref[rows_q, :]
            ku = kk[rows_k, :].astype(BF16)
            vu = vv[rows_k, :]
            first_key = jnp.where((i == 0) & (sb == 0), BAND, 0)
            ok = band_ok & (ki >= first_key)
            out = jnp.zeros((BAND, LANES), F32)
            lse = jnp.zeros((BAND, LANES), F32)
            for sel in head_lanes:
                qh = jnp.where(sel, qu, 0.0).astype(BF16)
                vh = jnp.where(sel, vu, 0.0).astype(BF16)
                s = lax.dot_general(qh, ku, (((1,), (1,)), ((), ())), preferred_element_type=F32)
                s = jnp.where(ok, s, NEG_INF)
                m = jnp.max(s, axis=-1, keepdims=True)
                p = jnp.exp(s - m)
                l = jnp.sum(p, axis=-1, keepdims=True)
                out = out + jnp.dot(p.astype(BF16), vh, preferred_element_type=F32) / l
                lse = jnp.where(sel, m + jnp.log(l), lse)
            oacc[rows_q, :] = out
            lacc[rows_q, :] = lse
            return carry

        lax.fori_loop(0, n_units, unit, 0)

    chunk = 256

    def combine(ci, carry):
        rows = pl.ds(pl.multiple_of(ci * chunk, chunk), chunk)
        l0, l1, l2 = lacc0[rows, :], lacc1[rows, :], lacc2[rows, :]
        m = jnp.maximum(jnp.maximum(l0, l1), l2)
        w0, w1, w2 = jnp.exp(l0 - m), jnp.exp(l1 - m), jnp.exp(l2 - m)
        num = w0 * oacc0[rows, :] + w1 * oacc1[rows, :] + w2 * oacc2[rows, :]
        o_ref[rows, :] = (num / (w0 + w1 + w2)).astype(BF16)
        return carry

    lax.fori_loop(0, ATT_TILE // chunk, combine, 0)


def _prompt_attn(qa, ka, va):
    b, _, t, _ = qa.shape
    n_i = t // ATT_TILE
    cur = pl.BlockSpec((None, None, ATT_TILE, LANES), lambda bi, hb, ii: (bi, hb, ii, 0))
    prev = pl.BlockSpec((None, None, ATT_TILE, LANES),
                        lambda bi, hb, ii: (bi, hb, jnp.maximum(ii - 1, 0), 0))
    acc = pltpu.VMEM((ATT_TILE, LANES), F32)
    return pl.pallas_call(
        _prompt_attn_body,
        grid=(b, N_HEAD_BLOCKS, n_i),
        in_specs=[cur, cur, prev, cur, prev],
        out_specs=pl.BlockSpec((None, ATT_TILE, LANES), lambda bi, hb, ii: (bi, ii, hb)),
        out_shape=jax.ShapeDtypeStruct((b, t, D_ATT), BF16),
        scratch_shapes=[pltpu.VMEM((2 * ATT_TILE, LANES), F32), pltpu.VMEM((2 * ATT_TILE, LANES), F32),
                        acc, acc, acc, acc, acc, acc],
        compiler_params=pltpu.CompilerParams(dimension_semantics=("arbitrary",) * 3,
                                             vmem_limit_bytes=VMEM_LIMIT),
        name="prompt_attn",
    )(qa, ka, ka, va, va)


def _out_ffn(x, ya, yb, wo_ref, gf_ref, wg_ref, wu_ref, wd_ref, cw_ref, cb_ref, gfin_ref, act_scr,
             shifted):
    x1 = (x + jnp.dot(ya, wo_ref[0:D_LRU, :], preferred_element_type=F32)
          + jnp.dot(yb, wo_ref[D_LRU:, :], preferred_element_type=F32))
    h = _rms(x1, gf_ref[...]).astype(BF16)
    for ci in range(D_FF // FF_CHUNK):
        cols = slice(ci * FF_CHUNK, (ci + 1) * FF_CHUNK)
        uf = jnp.dot(h, wg_ref[:, cols], preferred_element_type=F32)
        back1, back2 = shifted(uf, cols)
        uc = (cb_ref[:, cols] + back2 * cw_ref[0:1, cols] + back1 * cw_ref[1:2, cols]
              + uf * cw_ref[2:3, cols])
        up = jnp.dot(h, wu_ref[:, cols], preferred_element_type=F32)
        act_scr[:, cols] = (_gelu(uc) * up).astype(BF16)
    x2 = x1 + jnp.dot(act_scr[...], wd_ref[...], preferred_element_type=F32)
    return _rms(x2, gfin_ref[...])


def _prompt_out_body(x_ref, ya_ref, yb_ref, wo_ref, gf_ref, wg_ref, wu_ref, wd_ref, cw_ref, cb_ref,
                     gfin_ref, y_ref, ftail_ref, tail_scr, act_scr):
    t = pl.program_id(1)

    @pl.when(t == 0)
    def _():
        tail_scr[...] = jnp.zeros_like(tail_scr)

    def shifted(uf, cols):
        tail = tail_scr[:, cols]
        tail_scr[:, cols] = uf[uf.shape[0] - SUBLANES:]
        return _shift_rows(uf, tail, 1), _shift_rows(uf, tail, 2)

    y_ref[...] = _out_ffn(x_ref[...], ya_ref[...], yb_ref[...], wo_ref, gf_ref, wg_ref, wu_ref, wd_ref,
                          cw_ref, cb_ref, gfin_ref, act_scr, shifted)
    ftail_ref[...] = tail_scr[...]


def _prompt_out(x, ya, yb, w_out, g_ffn, w_gate, w_up, w_down, conv_w, conv_b, g_final):
    b, t, _ = x.shape
    whole = pl.BlockSpec(memory_space=pltpu.VMEM)

    def rows(width):
        return pl.BlockSpec((None, ROW_TILE, width), lambda bi, ti: (bi, ti, 0))

    return pl.pallas_call(
        _prompt_out_body,
        grid=(b, t // ROW_TILE),
        in_specs=[rows(D_MODEL), rows(D_LRU), rows(D_ATT)] + [whole] * 8,
        out_specs=[rows(D_MODEL), pl.BlockSpec((None, SUBLANES, D_FF), lambda bi, ti: (bi, 0, 0))],
        out_shape=[jax.ShapeDtypeStruct((b, t, D_MODEL), F32),
                   jax.ShapeDtypeStruct((b, SUBLANES, D_FF), F32)],
        scratch_shapes=[pltpu.VMEM((SUBLANES, D_FF), F32), pltpu.VMEM((ROW_TILE, D_FF), BF16)],
        compiler_params=pltpu.CompilerParams(dimension_semantics=("arbitrary", "arbitrary"),
                                             vmem_limit_bytes=VMEM_LIMIT),
        name="prompt_out",
    )(x, ya, yb, w_out, g_ffn, w_gate, w_up, w_down, conv_w, conv_b, g_final)


def _sample_in_body(n_b, n_s, x_ref, hist_ref, h0_ref, g_ref, win_ref, cw_ref, cb_ref, wgate_ref,
                    bgate_ref, lam_ref, c_ref, s1_ref, s2_ref,
                    ya_ref, q_ref, k_ref, v_ref, cst_ref, hlast_ref):
    z = _bdot(_rms(x_ref[...], g_ref[...]), win_ref[...])
    xa = z[:, :D_LRU]
    ga = z[:, D_LRU:2 * D_LRU]
    q = z[:, 2 * D_LRU:2 * D_LRU + D_ATT]
    k = z[:, 2 * D_LRU + D_ATT:2 * D_LRU + 2 * D_ATT]
    v = z[:, 2 * D_LRU + 2 * D_ATT:]

    ext = jnp.concatenate([hist_ref[...], xa], axis=0)
    xa_c = cb_ref[...] + ext[0:n_s * n_b] * cw_ref[0:1, :]
    for j in range(1, CONV_A):
        xa_c = xa_c + ext[j * n_b:(j + n_s) * n_b] * cw_ref[j:j + 1, :]
    cst_ref[...] = ext[n_s * n_b:]

    a, u = _lru_coeffs(xa_c, wgate_ref, bgate_ref, lam_ref)
    h = h0_ref[...]
    hs = []
    for s in range(n_s):
        h = a[s * n_b:(s + 1) * n_b] * h + u[s * n_b:(s + 1) * n_b]
        hs.append(h)
    hlast_ref[...] = h
    ya_ref[...] = (jnp.concatenate(hs, axis=0) * _gelu(ga)).astype(BF16)

    c, s1, s2 = c_ref[...], s1_ref[...], s2_ref[...]
    q_ref[...] = jnp.concatenate(_rope(q, c, s1, s2), axis=1) * ATTN_SCALE
    k_ref[...] = jnp.concatenate(_rope(k, c, s1, s2), axis=1)
    v_ref[...] = v


def _sample_in(x, hist, h0, g_mix, w_in, conv_w, conv_b, w_gate, b_gate, lam, tabs, n_b, n_s):
    rows = n_b * n_s
    att = jax.ShapeDtypeStruct((rows, D_ATT), F32)
    return pl.pallas_call(
        functools.partial(_sample_in_body, n_b, n_s),
        out_shape=[jax.ShapeDtypeStruct((rows, D_LRU), BF16), att, att, att,
                   jax.ShapeDtypeStruct(((CONV_A - 1) * n_b, D_LRU), F32),
                   jax.ShapeDtypeStruct((n_b, D_LRU), F32)],
        compiler_params=pltpu.CompilerParams(vmem_limit_bytes=VMEM_LIMIT),
        name="sample_in",
    )(x, hist, h0, g_mix, w_in, conv_w, conv_b, w_gate, b_gate, lam, *tabs)


def _sample_attn_body(q_ref, kn_ref, vn_ref, kt_ref, vt_ref, mult_ref, multn_ref, o_ref):
    mult = mult_ref[...]
    multn = multn_ref[...]
    nt = (((1,), (1,)), ((), ()))
    for h in range(N_HEADS):
        q = q_ref[h].astype(BF16)
        s = jnp.dot(q, kt_ref[h].astype(BF16), preferred_element_type=F32)
        s = jnp.where(mult > 0.0, s, NEG_INF)
        sn = lax.dot_general(q, kn_ref[h].astype(BF16), nt, preferred_element_type=F32)
        sn = jnp.where(multn > 0.0, sn, NEG_INF)
        m = jnp.maximum(jnp.max(s, axis=-1, keepdims=True), jnp.max(sn, axis=-1, keepdims=True))
        p = mult * jnp.exp(s - m)
        pn = multn * jnp.exp(sn - m)
        l = jnp.sum(p, axis=-1, keepdims=True) + jnp.sum(pn, axis=-1, keepdims=True)
        o = lax.dot_general(p.astype(BF16), vt_ref[h].astype(BF16), nt, preferred_element_type=F32)
        o = o + jnp.dot(pn.astype(BF16), vn_ref[h].astype(BF16), preferred_element_type=F32)
        o_ref[h] = o / l


def _pattern_counts(n_s, win):
    delta = win + jnp.arange(n_s)[:, None] - jnp.arange(win + n_s)[None, :]
    counts = sum(((delta >= 0) & (delta % d == 0) & (delta <= BAND * d)).astype(F32) for d in DILATIONS)
    return counts[:, :win], counts[:, win:]


def _sample_attn(q, k_new, v_new, cache_kt, cache_vt):
    n_b, _, n_s, _ = q.shape
    win = cache_kt.shape[-1]
    mult, multn = _pattern_counts(n_s, win)
    small = pl.BlockSpec((None, N_HEADS, n_s, HEAD_DIM), lambda bi: (bi, 0, 0, 0))
    cache = pl.BlockSpec((None, N_HEADS, HEAD_DIM, win), lambda bi: (bi, 0, 0, 0))
    whole = pl.BlockSpec(memory_space=pltpu.VMEM)
    return pl.pallas_call(
        _sample_attn_body,
        grid=(n_b,),
        in_specs=[small, small, small, cache, cache, whole, whole],
        out_specs=small,
        out_shape=jax.ShapeDtypeStruct((n_b, N_HEADS, n_s, HEAD_DIM), F32),
        compiler_params=pltpu.CompilerParams(dimension_semantics=("arbitrary",),
                                             vmem_limit_bytes=VMEM_LIMIT),
        name="sample_attn",
    )(q, k_new, v_new, cache_kt, cache_vt, mult, multn)


def _sample_out_body(n_b, x_ref, ya_ref, yb_ref, hist_ref, wo_ref, gf_ref, wg_ref, wu_ref, wd_ref,
                     cw_ref, cb_ref, gfin_ref, y_ref, fst_ref, act_scr):
    n_rows = x_ref.shape[0]

    def shifted(uf, cols):
        ext = jnp.concatenate([hist_ref[:, cols], uf], axis=0)
        fst_ref[:, cols] = ext[n_rows:]
        return ext[n_b:n_b + n_rows], ext[0:n_rows]

    y_ref[...] = _out_ffn(x_ref[...], ya_ref[...], yb_ref[...].astype(BF16), wo_ref, gf_ref, wg_ref,
                          wu_ref, wd_ref, cw_ref, cb_ref, gfin_ref, act_scr, shifted)


def _sample_out(x, ya, yb, hist, w_out, g_ffn, w_gate, w_up, w_down, conv_w, conv_b, g_final, n_b):
    rows = x.shape[0]
    return pl.pallas_call(
        functools.partial(_sample_out_body, n_b),
        out_shape=[jax.ShapeDtypeStruct((rows, D_MODEL), F32),
                   jax.ShapeDtypeStruct(((CONV_F - 1) * n_b, D_FF), F32)],
        scratch_shapes=[pltpu.VMEM((rows, D_FF), BF16)],
        compiler_params=pltpu.CompilerParams(vmem_limit_bytes=VMEM_LIMIT),
        name="sample_out",
    )(x, ya, yb, hist, w_out, g_ffn, w_gate, w_up, w_down, conv_w, conv_b, g_final)


def _rope_tables(pos):
    n = pos.shape[0]
    inv_freq = ROPE_THETA ** (-2.0 * jnp.arange(ROT_HALF, dtype=F32) / ROT_DIM)
    ang = pos[:, None] * inv_freq[None, :]
    cos, sin = jnp.cos(ang), jnp.sin(ang)
    rest = HEAD_DIM - ROT_DIM
    zeros_h = jnp.zeros((n, ROT_HALF), F32)
    c = jnp.concatenate([cos, cos, jnp.ones((n, rest), F32)], axis=1)
    s1 = jnp.concatenate([zeros_h, sin, jnp.zeros((n, rest), F32)], axis=1)
    s2 = jnp.concatenate([-sin, zeros_h, jnp.zeros((n, rest), F32)], axis=1)
    return tuple(jnp.tile(x, (1, HEADS_PER_BLOCK)) for x in (c, s1, s2))


def _block_diag(w):
    nb, n, _ = w.shape
    eye = jnp.eye(nb, dtype=w.dtype)
    return (eye[:, None, :, None] * w[:, :, None, :]).reshape(nb * n, nb * n)


def kernel(x_prompt, x_sample, state_conv_a, state_lru_h, cache_win_k, cache_win_v, state_conv_ffn,
           g_mix, w_in, conv_a_w, conv_a_b, lru_w_r, lru_b_r, lru_w_i, lru_b_i, lru_lambda, w_out,
           g_ffn, w_ffn_gate, conv_f_w, conv_f_b, w_ffn_up, w_ffn_down, g_final):
    n_p, seq_p, _ = x_prompt.shape
    n_s, seq_s, _ = x_sample.shape
    depth = w_in.shape[0]
    win = cache_win_k.shape[2]
    assert depth == 1
    assert seq_p % ATT_TILE == 0 and win == MAX_WINDOW and 2 * seq_s == SUBLANES

    tabs_p = _rope_tables(jnp.arange(seq_p, dtype=F32))
    tabs_s = tuple(jnp.repeat(x, n_s, axis=0)
                   for x in _rope_tables(PAST_LEN + jnp.arange(seq_s, dtype=F32)))
    g_fin = g_final.reshape(1, D_MODEL)

    xp = x_prompt
    xs = x_sample.transpose(1, 0, 2).reshape(seq_s * n_s, D_MODEL)
    outs = [[] for _ in range(10)]
    for layer in range(depth):
        w_in_b = w_in[layer].astype(BF16)
        w_gate = jnp.concatenate([_block_diag(lru_w_r[layer]), _block_diag(lru_w_i[layer])],
                                 axis=1).astype(BF16)
        b_gate = jnp.concatenate([lru_b_r[layer], lru_b_i[layer]]).reshape(1, 2 * D_LRU)
        lam = lru_lambda[layer].reshape(1, D_LRU)
        g_mix_l = g_mix[layer].reshape(1, D_MODEL)
        g_ffn_l = g_ffn[layer].reshape(1, D_MODEL)
        conv_a_b_l = conv_a_b[layer].reshape(1, D_LRU)
        conv_f_b_l = conv_f_b[layer].reshape(1, D_FF)
        w_out_b = w_out[layer].astype(BF16)
        w_fg = w_ffn_gate[layer].astype(BF16)
        w_fu = w_ffn_up[layer].astype(BF16)
        w_fd = w_ffn_down[layer].astype(BF16)

        ya, qa, ka, va, kn, vn, ctail, hlast = _prompt_in(
            xp, g_mix_l, w_in_b, conv_a_w[layer], conv_a_b_l, w_gate, b_gate, lam, tabs_p)
        yb = _prompt_attn(qa, ka, va)
        yp, ftail = _prompt_out(xp, ya, yb, w_out_b, g_ffn_l, w_fg, w_fu, w_fd, conv_f_w[layer],
                                conv_f_b_l, g_fin)
        outs[0].append(ctail[:, SUBLANES - (CONV_A - 1):])
        outs[1].append(hlast[:, 0])
        outs[2].append(kn.reshape(n_p, MAX_WINDOW, N_HEADS, HEAD_DIM))
        outs[3].append(vn.reshape(n_p, MAX_WINDOW, N_HEADS, HEAD_DIM))
        outs[4].append(ftail[:, SUBLANES - (CONV_F - 1):])

        hist_a = state_conv_a[layer].transpose(1, 0, 2).reshape((CONV_A - 1) * n_s, D_LRU)
        hist_f = state_conv_ffn[layer].transpose(1, 0, 2).reshape((CONV_F - 1) * n_s, D_FF)
        ya_s, q_s, k_s, v_s, cst_s, hlast_s = _sample_in(
            xs, hist_a, state_lru_h[layer], g_mix_l, w_in_b, conv_a_w[layer], conv_a_b_l, w_gate,
            b_gate, lam, tabs_s, n_s, seq_s)

        def per_head(x):
            return x.reshape(seq_s, n_s, N_HEADS, HEAD_DIM).transpose(1, 2, 0, 3)

        yb_s = _sample_attn(per_head(q_s), per_head(k_s), per_head(v_s),
                            cache_win_k[layer].transpose(0, 2, 3, 1),
                            cache_win_v[layer].transpose(0, 2, 3, 1))
        yb_s = yb_s.transpose(2, 0, 1, 3).reshape(seq_s * n_s, D_ATT)
        ys, fst_s = _sample_out(xs, ya_s, yb_s, hist_f, w_out_b, g_ffn_l, w_fg, w_fu, w_fd,
                                conv_f_w[layer], conv_f_b_l, g_fin, n_s)
        outs[5].append(cst_s.reshape(CONV_A - 1, n_s, D_LRU).transpose(1, 0, 2))
        outs[6].append(hlast_s)
        outs[7].append(k_s.reshape(seq_s, n_s, N_HEADS, HEAD_DIM).transpose(1, 0, 2, 3))
        outs[8].append(v_s.reshape(seq_s, n_s, N_HEADS, HEAD_DIM).transpose(1, 0, 2, 3))
        outs[9].append(fst_s.reshape(CONV_F - 1, n_s, D_FF).transpose(1, 0, 2))
        xp, xs = yp, ys

    y_sample = xs.reshape(seq_s, n_s, D_MODEL).transpose(1, 0, 2)
    return (xp, y_sample) + tuple(jnp.stack(o, axis=0) for o in outs)
```

```python
import functools

import jax
import jax.numpy as jnp
from jax import lax
from jax.experimental import pallas as pl
from jax.experimental.pallas import tpu as pltpu

F32 = jnp.float32
BF16 = jnp.bfloat16

D_MODEL = 1024
D_LRU = 512
N_LRU_BLOCKS = 8
LRU_BLOCK = D_LRU // N_LRU_BLOCKS
LRU_C = 8.0
CONV_A = 4
D_ATT = 512
N_HEADS = 8
HEAD_DIM = 64
ROT_DIM = 16
ROT_HALF = ROT_DIM // 2
ROPE_THETA = 500000.0
DILATIONS = (1, 4, 16)
BAND = 128
MAX_WINDOW = 2048
PAST_LEN = 8192
ATTN_SCALE = HEAD_DIM ** -0.5
LOG2_E = 1.4426950408889634
NEG_INF = -1e30
D_IN = 2 * D_LRU + 3 * D_ATT
D_FF = 3 * D_MODEL
CONV_F = 3
EPS = 1e-6

LANES = 128
SUBLANES = 8
HEADS_PER_BLOCK = LANES // HEAD_DIM
N_HEAD_BLOCKS = D_ATT // LANES
ATT_TILE = MAX_WINDOW
ROW_TILE = 512
FF_CHUNK = 512
VMEM_LIMIT = 56 * 1024 * 1024


def _rms(x, g):
    return x * lax.rsqrt(jnp.mean(x * x, axis=-1, keepdims=True) + EPS) * g


def _gelu(x):
    return 0.5 * x * (1.0 + jnp.tanh(0.7978845608028654 * (x + 0.044715 * (x * x * x))))


def _sigmoid(x):
    return 1.0 / (1.0 + jnp.exp(-x))


def _softplus(x):
    return jnp.maximum(x, 0.0) + jnp.log1p(jnp.exp(-jnp.abs(x)))


def _bdot(a, b):
    return jnp.dot(a.astype(BF16), b, preferred_element_type=F32)


def _shift_rows(x, tail, sh):
    rolled = pltpu.roll(x, sh, axis=0)
    rows = lax.broadcasted_iota(jnp.int32, tail.shape, 0)
    head = jnp.where(rows < sh, pltpu.roll(tail, sh, axis=0), rolled[0:SUBLANES])
    return jnp.concatenate([head, rolled[SUBLANES:]], axis=0)


def _rope(x, c, s1, s2):
    out = []
    for hb in range(N_HEAD_BLOCKS):
        blk = x[:, hb * LANES:(hb + 1) * LANES]
        out.append(blk * c + pltpu.roll(blk, ROT_HALF, axis=1) * s1
                   + pltpu.roll(blk, LANES - ROT_HALF, axis=1) * s2)
    return out


def _lru_coeffs(xa_c, wgate_ref, bgate_ref, lam_ref):
    gates = _bdot(xa_c, wgate_ref[...]) + bgate_ref[...]
    r = _sigmoid(gates[:, :D_LRU])
    gi = _sigmoid(gates[:, D_LRU:])
    log_a = -LRU_C * r * _softplus(-lam_ref[...])
    a = jnp.exp(log_a)
    u = jnp.sqrt(jnp.tanh(-log_a) * (1.0 + a * a)) * (gi * xa_c)
    return a, u


def _lru_scan(a, u, h0):
    n = a.shape[0]
    g = n // SUBLANES
    a3 = a.reshape(g, SUBLANES, D_LRU)
    u3 = u.reshape(g, SUBLANES, D_LRU)
    rows = lax.broadcasted_iota(jnp.int32, a3.shape, 1)
    sh = 1
    while sh < SUBLANES:
        keep = rows >= sh
        a_s = jnp.where(keep, pltpu.roll(a3, sh, axis=1), 1.0)
        u_s = jnp.where(keep, pltpu.roll(u3, sh, axis=1), 0.0)
        u3 = a3 * u_s + u3
        a3 = a3 * a_s
        sh *= 2
    carry = h0
    hs = []
    for i in range(g):
        hg = u3[i] + a3[i] * carry
        carry = hg[SUBLANES - 1:SUBLANES]
        hs.append(hg)
    return jnp.concatenate(hs, axis=0), carry


def _prompt_in_body(n_t, x_ref, g_ref, win_ref, cw_ref, cb_ref, wgate_ref, bgate_ref, lam_ref,
                    c_ref, s1_ref, s2_ref,
                    ya_ref, qa_ref, ka_ref, va_ref, kn_ref, vn_ref, ctail_ref, hlast_ref,
                    tail_scr, h_scr):
    t = pl.program_id(1)

    @pl.when(t == 0)
    def _():
        tail_scr[...] = jnp.zeros_like(tail_scr)
        h_scr[...] = jnp.zeros_like(h_scr)

    x = x_ref[...]
    z = _bdot(_rms(x, g_ref[...]), win_ref[...])
    xa = z[:, :D_LRU]
    ga = z[:, D_LRU:2 * D_LRU]
    q = z[:, 2 * D_LRU:2 * D_LRU + D_ATT]
    k = z[:, 2 * D_LRU + D_ATT:2 * D_LRU + 2 * D_ATT]
    v = z[:, 2 * D_LRU + 2 * D_ATT:]

    tail = tail_scr[...]
    xa_c = cb_ref[...] + xa * cw_ref[CONV_A - 1:CONV_A, :]
    for j in range(CONV_A - 1):
        xa_c = xa_c + _shift_rows(xa, tail, CONV_A - 1 - j) * cw_ref[j:j + 1, :]
    new_tail = xa[xa.shape[0] - SUBLANES:]
    tail_scr[...] = new_tail
    ctail_ref[...] = new_tail

    a, u = _lru_coeffs(xa_c, wgate_ref, bgate_ref, lam_ref)
    hs, carry = _lru_scan(a, u, h_scr[...])
    h_scr[...] = carry
    hlast_ref[...] = carry
    ya_ref[...] = (hs * _gelu(ga)).astype(BF16)

    c, s1, s2 = c_ref[...], s1_ref[...], s2_ref[...]
    q_blocks = _rope(q, c, s1, s2)
    k_blocks = _rope(k, c, s1, s2)
    for hb in range(N_HEAD_BLOCKS):
        qa_ref[hb] = q_blocks[hb] * (ATTN_SCALE * LOG2_E)
        ka_ref[hb] = k_blocks[hb]
        va_ref[hb] = v[:, hb * LANES:(hb + 1) * LANES]

    @pl.when(t >= n_t - MAX_WINDOW // ROW_TILE)
    def _():
        kn_ref[...] = jnp.concatenate(k_blocks, axis=1)
        vn_ref[...] = v


def _prompt_in(x, g_mix, w_in, conv_w, conv_b, w_gate, b_gate, lam, tabs):
    b, t, _ = x.shape
    n_t = t // ROW_TILE
    keep_tiles = MAX_WINDOW // ROW_TILE
    whole = pl.BlockSpec(memory_space=pltpu.VMEM)
    tab_spec = pl.BlockSpec((ROW_TILE, LANES), lambda bi, ti: (ti, 0))
    att_spec = pl.BlockSpec((None, N_HEAD_BLOCKS, ROW_TILE, LANES), lambda bi, ti: (bi, 0, ti, 0))
    nat_spec = pl.BlockSpec((None, ROW_TILE, D_ATT),
                            lambda bi, ti: (bi, jnp.maximum(ti - (n_t - keep_tiles), 0), 0))
    att_shape = jax.ShapeDtypeStruct((b, N_HEAD_BLOCKS, t, LANES), F32)
    nat_shape = jax.ShapeDtypeStruct((b, MAX_WINDOW, D_ATT), F32)
    return pl.pallas_call(
        functools.partial(_prompt_in_body, n_t),
        grid=(b, n_t),
        in_specs=[pl.BlockSpec((None, ROW_TILE, D_MODEL), lambda bi, ti: (bi, ti, 0)),
                  whole, whole, whole, whole, whole, whole, whole, tab_spec, tab_spec, tab_spec],
        out_specs=[pl.BlockSpec((None, ROW_TILE, D_LRU), lambda bi, ti: (bi, ti, 0)),
                   att_spec, att_spec, att_spec, nat_spec, nat_spec,
                   pl.BlockSpec((None, SUBLANES, D_LRU), lambda bi, ti: (bi, 0, 0)),
                   pl.BlockSpec((None, 1, D_LRU), lambda bi, ti: (bi, 0, 0))],
        out_shape=[jax.ShapeDtypeStruct((b, t, D_LRU), BF16), att_shape, att_shape, att_shape,
                   nat_shape, nat_shape,
                   jax.ShapeDtypeStruct((b, SUBLANES, D_LRU), F32),
                   jax.ShapeDtypeStruct((b, 1, D_LRU), F32)],
        scratch_shapes=[pltpu.VMEM((SUBLANES, D_LRU), F32), pltpu.VMEM((1, D_LRU), F32)],
        compiler_params=pltpu.CompilerParams(dimension_semantics=("arbitrary", "arbitrary"),
                                             vmem_limit_bytes=VMEM_LIMIT),
        name="prompt_in",
    )(x, g_mix, w_in, conv_w, conv_b, w_gate, b_gate, lam, *tabs)


def _prompt_attn_body(q_ref, kc_ref, kp_ref, vc_ref, vp_ref, o_ref, kk, vv, bias, *accs):
    i = pl.program_id(2)
    kk[0:ATT_TILE] = kp_ref[...]
    kk[ATT_TILE:] = kc_ref[...]
    vv[0:ATT_TILE] = vp_ref[...]
    vv[ATT_TILE:] = vc_ref[...]

    qi = lax.broadcasted_iota(jnp.int32, (BAND, 2 * BAND), 0)
    ki = lax.broadcasted_iota(jnp.int32, (BAND, 2 * BAND), 1)
    band_ok = (ki >= qi) & (ki <= qi + BAND)
    bias[0] = jnp.where(band_ok, 0.0, NEG_INF)
    bias[1] = jnp.where(band_ok & (ki >= BAND), 0.0, NEG_INF)

    lane = lax.broadcasted_iota(jnp.int32, (1, LANES), 1)
    head0 = lane < HEAD_DIM
    n_units = ATT_TILE // BAND
    nt = (((1,), (1,)), ((), ()))

    for pi, d in enumerate(DILATIONS):
        oacc, macc, lacc = accs[3 * pi:3 * pi + 3]

        def unit(un, carry, d=d, oacc=oacc, macc=macc, lacc=lacc):
            sb = un // d
            base_q = sb * (BAND * d) + un % d
            rows_q = pl.ds(base_q, BAND, stride=d)
            rows_k = pl.ds(ATT_TILE + base_q - BAND * d, 2 * BAND, stride=d)
            qb = q_ref[rows_q, :].astype(BF16)
            kb = kk[rows_k, :].astype(BF16)
            vb = vv[rows_k, :].astype(BF16)
            mask = bias[jnp.where((i == 0) & (sb == 0), 1, 0)]
            res, mx = [], []
            for own in (head0, ~head0):
                qh = jnp.where(own, qb, jnp.zeros_like(qb))
                vh = jnp.where(own, vb, jnp.ones_like(vb))
                s = lax.dot_general(qh, kb, nt, preferred_element_type=F32) + mask
                m = jnp.max(s, axis=-1, keepdims=True)
                p = jnp.exp2(s - m)
                res.append(jnp.dot(p.astype(BF16), vh, preferred_element_type=F32))
                mx.append(m)
            oacc[rows_q, :] = jnp.where(head0, res[0], res[1])
            macc[rows_q, :] = jnp.where(head0, mx[0], mx[1])
            lacc[rows_q, :] = pltpu.roll(jnp.where(head0, res[1], res[0]), HEAD_DIM, axis=1)
            return carry

        lax.fori_loop(0, n_units, unit, 0, unroll=8)

    chunk = 256

    def combine(ci, carry):
        rows = pl.ds(pl.multiple_of(ci * chunk, chunk), chunk)
        ms = [accs[3 * pi + 1][rows, :] for pi in range(len(DILATIONS))]
        top = functools.reduce(jnp.maximum, ms)
        ws = [jnp.exp2(m - top) for m in ms]
        num = sum(w * accs[3 * pi][rows, :] for pi, w in enumerate(ws))
        den = sum(w * accs[3 * pi + 2][rows, :] for pi, w in enumerate(ws))
        o_ref[rows, :] = (num / den).astype(BF16)
        return carry

    lax.fori_loop(0, ATT_TILE // chunk, combine, 0)


def _prompt_attn(qa, ka, va):
    b, _, t, _ = qa.shape
    n_i = t // ATT_TILE
    cur = pl.BlockSpec((None, None, ATT_TILE, LANES), lambda bi, hb, ii: (bi, hb, ii, 0))
    prev = pl.BlockSpec((None, None, ATT_TILE, LANES),
                        lambda bi, hb, ii: (bi, hb, jnp.maximum(ii - 1, 0), 0))
    acc = pltpu.VMEM((ATT_TILE, LANES), F32)
    return pl.pallas_call(
        _prompt_attn_body,
        grid=(b, N_HEAD_BLOCKS, n_i),
        in_specs=[cur, cur, prev, cur, prev],
        out_specs=pl.BlockSpec((None, ATT_TILE, LANES), lambda bi, hb, ii: (bi, ii, hb)),
        out_shape=jax.ShapeDtypeStruct((b, t, D_ATT), BF16),
        scratch_shapes=[pltpu.VMEM((2 * ATT_TILE, LANES), F32), pltpu.VMEM((2 * ATT_TILE, LANES), F32),
                        pltpu.VMEM((2, BAND, 2 * BAND), F32)] + [acc] * (3 * len(DILATIONS)),
        compiler_params=pltpu.CompilerParams(dimension_semantics=("arbitrary",) * 3,
                                             vmem_limit_bytes=VMEM_LIMIT),
        name="prompt_attn",
    )(qa, ka, ka, va, va)


def _out_ffn(x, ya, yb, wo_ref, gf_ref, wg_ref, wu_ref, wd_ref, cw_ref, cb_ref, gfin_ref, act_scr,
             shifted):
    x1 = (x + jnp.dot(ya, wo_ref[0:D_LRU, :], preferred_element_type=F32)
          + jnp.dot(yb, wo_ref[D_LRU:, :], preferred_element_type=F32))
    h = _rms(x1, gf_ref[...]).astype(BF16)
    for ci in range(D_FF // FF_CHUNK):
        cols = slice(ci * FF_CHUNK, (ci + 1) * FF_CHUNK)
        uf = jnp.dot(h, wg_ref[:, cols], preferred_element_type=F32)
        back1, back2 = shifted(uf, cols)
        uc = (cb_ref[:, cols] + back2 * cw_ref[0:1, cols] + back1 * cw_ref[1:2, cols]
              + uf * cw_ref[2:3, cols])
        up = jnp.dot(h, wu_ref[:, cols], preferred_element_type=F32)
        act_scr[:, cols] = (_gelu(uc) * up).astype(BF16)
    x2 = x1 + jnp.dot(act_scr[...], wd_ref[...], preferred_element_type=F32)
    return _rms(x2, gfin_ref[...])


def _prompt_out_body(x_ref, ya_ref, yb_ref, wo_ref, gf_ref, wg_ref, wu_ref, wd_ref, cw_ref, cb_ref,
                     gfin_ref, y_ref, ftail_ref, tail_scr, act_scr):
    t = pl.program_id(1)

    @pl.when(t == 0)
    def _():
        tail_scr[...] = jnp.zeros_like(tail_scr)

    def shifted(uf, cols):
        tail = tail_scr[:, cols]
        tail_scr[:, cols] = uf[uf.shape[0] - SUBLANES:]
        return _shift_rows(uf, tail, 1), _shift_rows(uf, tail, 2)

    y_ref[...] = _out_ffn(x_ref[...], ya_ref[...], yb_ref[...], wo_ref, gf_ref, wg_ref, wu_ref, wd_ref,
                          cw_ref, cb_ref, gfin_ref, act_scr, shifted)
    ftail_ref[...] = tail_scr[...]


def _prompt_out(x, ya, yb, w_out, g_ffn, w_gate, w_up, w_down, conv_w, conv_b, g_final):
    b, t, _ = x.shape
    whole = pl.BlockSpec(memory_space=pltpu.VMEM)

    def rows(width):
        return pl.BlockSpec((None, ROW_TILE, width), lambda bi, ti: (bi, ti, 0))

    return pl.pallas_call(
        _prompt_out_body,
        grid=(b, t // ROW_TILE),
        in_specs=[rows(D_MODEL), rows(D_LRU), rows(D_ATT)] + [whole] * 8,
        out_specs=[rows(D_MODEL), pl.BlockSpec((None, SUBLANES, D_FF), lambda bi, ti: (bi, 0, 0))],
        out_shape=[jax.ShapeDtypeStruct((b, t, D_MODEL), F32),
                   jax.ShapeDtypeStruct((b, SUBLANES, D_FF), F32)],
        scratch_shapes=[pltpu.VMEM((SUBLANES, D_FF), F32), pltpu.VMEM((ROW_TILE, D_FF), BF16)],
        compiler_params=pltpu.CompilerParams(dimension_semantics=("arbitrary", "arbitrary"),
                                             vmem_limit_bytes=VMEM_LIMIT),
        name="prompt_out",
    )(x, ya, yb, w_out, g_ffn, w_gate, w_up, w_down, conv_w, conv_b, g_final)


def _sample_in_body(n_b, n_s, x_ref, hist_ref, h0_ref, g_ref, win_ref, cw_ref, cb_ref, wgate_ref,
                    bgate_ref, lam_ref, c_ref, s1_ref, s2_ref,
                    ya_ref, q_ref, k_ref, v_ref, cst_ref, hlast_ref):
    z = _bdot(_rms(x_ref[...], g_ref[...]), win_ref[...])
    xa = z[:, :D_LRU]
    ga = z[:, D_LRU:2 * D_LRU]
    q = z[:, 2 * D_LRU:2 * D_LRU + D_ATT]
    k = z[:, 2 * D_LRU + D_ATT:2 * D_LRU + 2 * D_ATT]
    v = z[:, 2 * D_LRU + 2 * D_ATT:]

    ext = jnp.concatenate([hist_ref[...], xa], axis=0)
    xa_c = cb_ref[...] + ext[0:n_s * n_b] * cw_ref[0:1, :]
    for j in range(1, CONV_A):
        xa_c = xa_c + ext[j * n_b:(j + n_s) * n_b] * cw_ref[j:j + 1, :]
    cst_ref[...] = ext[n_s * n_b:]

    a, u = _lru_coeffs(xa_c, wgate_ref, bgate_ref, lam_ref)
    h = h0_ref[...]
    hs = []
    for s in range(n_s):
        h = a[s * n_b:(s + 1) * n_b] * h + u[s * n_b:(s + 1) * n_b]
        hs.append(h)
    hlast_ref[...] = h
    ya_ref[...] = (jnp.concatenate(hs, axis=0) * _gelu(ga)).astype(BF16)

    c, s1, s2 = c_ref[...], s1_ref[...], s2_ref[...]
    q_ref[...] = jnp.concatenate(_rope(q, c, s1, s2), axis=1) * ATTN_SCALE
    k_ref[...] = jnp.concatenate(_rope(k, c, s1, s2), axis=1)
    v_ref[...] = v


def _sample_in(x, hist, h0, g_mix, w_in, conv_w, conv_b, w_gate, b_gate, lam, tabs, n_b, n_s):
    rows = n_b * n_s
    att = jax.ShapeDtypeStruct((rows, D_ATT), F32)
    return pl.pallas_call(
        functools.partial(_sample_in_body, n_b, n_s),
        out_shape=[jax.ShapeDtypeStruct((rows, D_LRU), BF16), att, att, att,
                   jax.ShapeDtypeStruct(((CONV_A - 1) * n_b, D_LRU), F32),
                   jax.ShapeDtypeStruct((n_b, D_LRU), F32)],
        compiler_params=pltpu.CompilerParams(vmem_limit_bytes=VMEM_LIMIT),
        name="sample_in",
    )(x, hist, h0, g_mix, w_in, conv_w, conv_b, w_gate, b_gate, lam, *tabs)


def _sample_attn_body(q_ref, kn_ref, vn_ref, kt_ref, vt_ref, mult_ref, multn_ref, o_ref):
    mult = mult_ref[...]
    multn = multn_ref[...]
    nt = (((1,), (1,)), ((), ()))
    for h in range(N_HEADS):
        q = q_ref[h].astype(BF16)
        s = jnp.dot(q, kt_ref[h].astype(BF16), preferred_element_type=F32)
        s = jnp.where(mult > 0.0, s, NEG_INF)
        sn = lax.dot_general(q, kn_ref[h].astype(BF16), nt, preferred_element_type=F32)
        sn = jnp.where(multn > 0.0, sn, NEG_INF)
        m = jnp.maximum(jnp.max(s, axis=-1, keepdims=True), jnp.max(sn, axis=-1, keepdims=True))
        p = mult * jnp.exp(s - m)
        pn = multn * jnp.exp(sn - m)
        l = jnp.sum(p, axis=-1, keepdims=True) + jnp.sum(pn, axis=-1, keepdims=True)
        o = lax.dot_general(p.astype(BF16), vt_ref[h].astype(BF16), nt, preferred_element_type=F32)
        o = o + jnp.dot(pn.astype(BF16), vn_ref[h].astype(BF16), preferred_element_type=F32)
        o_ref[h] = o / l


def _pattern_counts(n_s, win):
    delta = win + jnp.arange(n_s)[:, None] - jnp.arange(win + n_s)[None, :]
    counts = sum(((delta >= 0) & (delta % d == 0) & (delta <= BAND * d)).astype(F32) for d in DILATIONS)
    return counts[:, :win], counts[:, win:]


def _sample_attn(q, k_new, v_new, cache_kt, cache_vt):
    n_b, _, n_s, _ = q.shape
    win = cache_kt.shape[-1]
    mult, multn = _pattern_counts(n_s, win)
    small = pl.BlockSpec((None, N_HEADS, n_s, HEAD_DIM), lambda bi: (bi, 0, 0, 0))
    cache = pl.BlockSpec((None, N_HEADS, HEAD_DIM, win), lambda bi: (bi, 0, 0, 0))
    whole = pl.BlockSpec(memory_space=pltpu.VMEM)
    return pl.pallas_call(
        _sample_attn_body,
        grid=(n_b,),
        in_specs=[small, small, small, cache, cache, whole, whole],
        out_specs=small,
        out_shape=jax.ShapeDtypeStruct((n_b, N_HEADS, n_s, HEAD_DIM), F32),
        compiler_params=pltpu.CompilerParams(dimension_semantics=("arbitrary",),
                                             vmem_limit_bytes=VMEM_LIMIT),
        name="sample_attn",
    )(q, k_new, v_new, cache_kt, cache_vt, mult, multn)


def _sample_out_body(n_b, x_ref, ya_ref, yb_ref, hist_ref, wo_ref, gf_ref, wg_ref, wu_ref, wd_ref,
                     cw_ref, cb_ref, gfin_ref, y_ref, fst_ref, act_scr):
    n_rows = x_ref.shape[0]

    def shifted(uf, cols):
        ext = jnp.concatenate([hist_ref[:, cols], uf], axis=0)
        fst_ref[:, cols] = ext[n_rows:]
        return ext[n_b:n_b + n_rows], ext[0:n_rows]

    y_ref[...] = _out_ffn(x_ref[...], ya_ref[...], yb_ref[...].astype(BF16), wo_ref, gf_ref, wg_ref,
                          wu_ref, wd_ref, cw_ref, cb_ref, gfin_ref, act_scr, shifted)


def _sample_out(x, ya, yb, hist, w_out, g_ffn, w_gate, w_up, w_down, conv_w, conv_b, g_final, n_b):
    rows = x.shape[0]
    return pl.pallas_call(
        functools.partial(_sample_out_body, n_b),
        out_shape=[jax.ShapeDtypeStruct((rows, D_MODEL), F32),
                   jax.ShapeDtypeStruct(((CONV_F - 1) * n_b, D_FF), F32)],
        scratch_shapes=[pltpu.VMEM((rows, D_FF), BF16)],
        compiler_params=pltpu.CompilerParams(vmem_limit_bytes=VMEM_LIMIT),
        name="sample_out",
    )(x, ya, yb, hist, w_out, g_ffn, w_gate, w_up, w_down, conv_w, conv_b, g_final)


def _rope_tables(pos):
    n = pos.shape[0]
    inv_freq = ROPE_THETA ** (-2.0 * jnp.arange(ROT_HALF, dtype=F32) / ROT_DIM)
    ang = pos[:, None] * inv_freq[None, :]
    cos, sin = jnp.cos(ang), jnp.sin(ang)
    rest = HEAD_DIM - ROT_DIM
    zeros_h = jnp.zeros((n, ROT_HALF), F32)
    c = jnp.concatenate([cos, cos, jnp.ones((n, rest), F32)], axis=1)
    s1 = jnp.concatenate([zeros_h, sin, jnp.zeros((n, rest), F32)], axis=1)
    s2 = jnp.concatenate([-sin, zeros_h, jnp.zeros((n, rest), F32)], axis=1)
    return tuple(jnp.tile(x, (1, HEADS_PER_BLOCK)) for x in (c, s1, s2))


def _block_diag(w):
    nb, n, _ = w.shape
    eye = jnp.eye(nb, dtype=w.dtype)
    return (eye[:, None, :, None] * w[:, :, None, :]).reshape(nb * n, nb * n)


def kernel(x_prompt, x_sample, state_conv_a, state_lru_h, cache_win_k, cache_win_v, state_conv_ffn,
           g_mix, w_in, conv_a_w, conv_a_b, lru_w_r, lru_b_r, lru_w_i, lru_b_i, lru_lambda, w_out,
           g_ffn, w_ffn_gate, conv_f_w, conv_f_b, w_ffn_up, w_ffn_down, g_final):
    n_p, seq_p, _ = x_prompt.shape
    n_s, seq_s, _ = x_sample.shape
    depth = w_in.shape[0]
    win = cache_win_k.shape[2]
    assert depth == 1
    assert seq_p % ATT_TILE == 0 and win == MAX_WINDOW and 2 * seq_s == SUBLANES

    tabs_p = _rope_tables(jnp.arange(seq_p, dtype=F32))
    tabs_s = tuple(jnp.repeat(x, n_s, axis=0)
                   for x in _rope_tables(PAST_LEN + jnp.arange(seq_s, dtype=F32)))
    g_fin = g_final.reshape(1, D_MODEL)

    xp = x_prompt
    xs = x_sample.transpose(1, 0, 2).reshape(seq_s * n_s, D_MODEL)
    outs = [[] for _ in range(10)]
    for layer in range(depth):
        w_in_b = w_in[layer].astype(BF16)
        w_gate = jnp.concatenate([_block_diag(lru_w_r[layer]), _block_diag(lru_w_i[layer])],
                                 axis=1).astype(BF16)
        b_gate = jnp.concatenate([lru_b_r[layer], lru_b_i[layer]]).reshape(1, 2 * D_LRU)
        lam = lru_lambda[layer].reshape(1, D_LRU)
        g_mix_l = g_mix[layer].reshape(1, D_MODEL)
        g_ffn_l = g_ffn[layer].reshape(1, D_MODEL)
        conv_a_b_l = conv_a_b[layer].reshape(1, D_LRU)
        conv_f_b_l = conv_f_b[layer].reshape(1, D_FF)
        w_out_b = w_out[layer].astype(BF16)
        w_fg = w_ffn_gate[layer].astype(BF16)
        w_fu = w_ffn_up[layer].astype(BF16)
        w_fd = w_ffn_down[layer].astype(BF16)

        ya, qa, ka, va, kn, vn, ctail, hlast = _prompt_in(
            xp, g_mix_l, w_in_b, conv_a_w[layer], conv_a_b_l, w_gate, b_gate, lam, tabs_p)
        yb = _prompt_attn(qa, ka, va)
        yp, ftail = _prompt_out(xp, ya, yb, w_out_b, g_ffn_l, w_fg, w_fu, w_fd, conv_f_w[layer],
                                conv_f_b_l, g_fin)
        outs[0].append(ctail[:, SUBLANES - (CONV_A - 1):])
        outs[1].append(hlast[:, 0])
        outs[2].append(kn.reshape(n_p, MAX_WINDOW, N_HEADS, HEAD_DIM))
        outs[3].append(vn.reshape(n_p, MAX_WINDOW, N_HEADS, HEAD_DIM))
        outs[4].append(ftail[:, SUBLANES - (CONV_F - 1):])

        hist_a = state_conv_a[layer].transpose(1, 0, 2).reshape((CONV_A - 1) * n_s, D_LRU)
        hist_f = state_conv_ffn[layer].transpose(1, 0, 2).reshape((CONV_F - 1) * n_s, D_FF)
        ya_s, q_s, k_s, v_s, cst_s, hlast_s = _sample_in(
            xs, hist_a, state_lru_h[layer], g_mix_l, w_in_b, conv_a_w[layer], conv_a_b_l, w_gate,
            b_gate, lam, tabs_s, n_s, seq_s)

        def per_head(x):
            return x.reshape(seq_s, n_s, N_HEADS, HEAD_DIM).transpose(1, 2, 0, 3)

        yb_s = _sample_attn(per_head(q_s), per_head(k_s), per_head(v_s),
                            cache_win_k[layer].transpose(0, 2, 3, 1),
                            cache_win_v[layer].transpose(0, 2, 3, 1))
        yb_s = yb_s.transpose(2, 0, 1, 3).reshape(seq_s * n_s, D_ATT)
        ys, fst_s = _sample_out(xs, ya_s, yb_s, hist_f, w_out_b, g_ffn_l, w_fg, w_fu, w_fd,
                                conv_f_w[layer], conv_f_b_l, g_fin, n_s)
        outs[5].append(cst_s.reshape(CONV_A - 1, n_s, D_LRU).transpose(1, 0, 2))
        outs[6].append(hlast_s)
        outs[7].append(k_s.reshape(seq_s, n_s, N_HEADS, HEAD_DIM).transpose(1, 0, 2, 3))
        outs[8].append(v_s.reshape(seq_s, n_s, N_HEADS, HEAD_DIM).transpose(1, 0, 2, 3))
        outs[9].append(fst_s.reshape(CONV_F - 1, n_s, D_FF).transpose(1, 0, 2))
        xp, xs = yp, ys

    y_sample = xs.reshape(seq_s, n_s, D_MODEL).transpose(1, 0, 2)
    return (xp, y_sample) + tuple(jnp.stack(o, axis=0) for o in outs)
```

```python
import functools

import jax
import jax.numpy as jnp
import numpy as np
from jax import lax
from jax.experimental import pallas as pl
from jax.experimental.pallas import tpu as pltpu

F32 = jnp.float32
BF16 = jnp.bfloat16

D_MODEL = 1024
D_LRU = 512
N_LRU_BLOCKS = 8
LRU_BLOCK = D_LRU // N_LRU_BLOCKS
LRU_C = 8.0
CONV_A = 4
D_ATT = 512
N_HEADS = 8
HEAD_DIM = 64
ROT_DIM = 16
ROT_HALF = ROT_DIM // 2
ROPE_THETA = 500000.0
DILATIONS = (1, 4, 16)
BAND = 128
MAX_WINDOW = 2048
PAST_LEN = 8192
ATTN_SCALE = HEAD_DIM ** -0.5
LOG2_E = 1.4426950408889634
NEG_INF = -1e30
D_IN = 2 * D_LRU + 3 * D_ATT
D_FF = 3 * D_MODEL
CONV_F = 3
EPS = 1e-6

LANES = 128
SUBLANES = 8
HEADS_PER_BLOCK = LANES // HEAD_DIM
N_HEAD_BLOCKS = D_ATT // LANES
ATT_TILE = MAX_WINDOW
ROW_TILE = 512
FF_CHUNK = 512
VMEM_LIMIT = 56 * 1024 * 1024


def _rms(x, g):
    return x * lax.rsqrt(jnp.mean(x * x, axis=-1, keepdims=True) + EPS) * g


def _gelu(x):
    return 0.5 * x * (1.0 + jnp.tanh(0.7978845608028654 * (x + 0.044715 * (x * x * x))))


def _sigmoid(x):
    return 1.0 / (1.0 + jnp.exp(-x))


def _softplus(x):
    return jnp.maximum(x, 0.0) + jnp.log1p(jnp.exp(-jnp.abs(x)))


def _bdot(a, b):
    return jnp.dot(a.astype(BF16), b, preferred_element_type=F32)


def _shift_rows(x, tail, sh):
    rolled = pltpu.roll(x, sh, axis=0)
    rows = lax.broadcasted_iota(jnp.int32, tail.shape, 0)
    head = jnp.where(rows < sh, pltpu.roll(tail, sh, axis=0), rolled[0:SUBLANES])
    return jnp.concatenate([head, rolled[SUBLANES:]], axis=0)


def _rope(x, c, s1, s2):
    out = []
    for hb in range(N_HEAD_BLOCKS):
        blk = x[:, hb * LANES:(hb + 1) * LANES]
        out.append(blk * c + pltpu.roll(blk, ROT_HALF, axis=1) * s1
                   + pltpu.roll(blk, LANES - ROT_HALF, axis=1) * s2)
    return out


def _lru_coeffs(xa_c, wgate_ref, bgate_ref, lam_ref):
    gates = _bdot(xa_c, wgate_ref[...]) + bgate_ref[...]
    r = _sigmoid(gates[:, :D_LRU])
    gi = _sigmoid(gates[:, D_LRU:])
    log_a = -LRU_C * r * _softplus(-lam_ref[...])
    a = jnp.exp(log_a)
    u = jnp.sqrt(jnp.tanh(-log_a) * (1.0 + a * a)) * (gi * xa_c)
    return a, u


def _lru_scan(a, u, h0):
    n = a.shape[0]
    g = n // SUBLANES
    a3 = a.reshape(g, SUBLANES, D_LRU)
    u3 = u.reshape(g, SUBLANES, D_LRU)
    rows = lax.broadcasted_iota(jnp.int32, a3.shape, 1)
    sh = 1
    while sh < SUBLANES:
        keep = rows >= sh
        a_s = jnp.where(keep, pltpu.roll(a3, sh, axis=1), 1.0)
        u_s = jnp.where(keep, pltpu.roll(u3, sh, axis=1), 0.0)
        u3 = a3 * u_s + u3
        a3 = a3 * a_s
        sh *= 2
    carry = h0
    hs = []
    for i in range(g):
        hg = u3[i] + a3[i] * carry
        carry = hg[SUBLANES - 1:SUBLANES]
        hs.append(hg)
    return jnp.concatenate(hs, axis=0), carry


def _prompt_in_body(n_t, x_ref, g_ref, win_ref, cw_ref, cb_ref, wgate_ref, bgate_ref, lam_ref,
                    c_ref, s1_ref, s2_ref,
                    ya_ref, qa_ref, ka_ref, va_ref, kn_ref, vn_ref, ctail_ref, hlast_ref,
                    tail_scr, h_scr, perm_scr):
    t = pl.program_id(1)

    @pl.when(t == 0)
    def _():
        tail_scr[...] = jnp.zeros_like(tail_scr)
        h_scr[...] = jnp.zeros_like(h_scr)

    x = x_ref[...]
    z = _bdot(_rms(x, g_ref[...]), win_ref[...])
    xa = z[:, :D_LRU]
    ga = z[:, D_LRU:2 * D_LRU]
    q = z[:, 2 * D_LRU:2 * D_LRU + D_ATT]
    k = z[:, 2 * D_LRU + D_ATT:2 * D_LRU + 2 * D_ATT]
    v = z[:, 2 * D_LRU + 2 * D_ATT:]

    tail = tail_scr[...]
    xa_c = cb_ref[...] + xa * cw_ref[CONV_A - 1:CONV_A, :]
    for j in range(CONV_A - 1):
        xa_c = xa_c + _shift_rows(xa, tail, CONV_A - 1 - j) * cw_ref[j:j + 1, :]
    new_tail = xa[xa.shape[0] - SUBLANES:]
    tail_scr[...] = new_tail
    ctail_ref[...] = new_tail

    a, u = _lru_coeffs(xa_c, wgate_ref, bgate_ref, lam_ref)
    hs, carry = _lru_scan(a, u, h_scr[...])
    h_scr[...] = carry
    hlast_ref[...] = carry
    ya_ref[...] = (hs * _gelu(ga)).astype(BF16)

    c, s1, s2 = c_ref[...], s1_ref[...], s2_ref[...]
    q_blocks = _rope(q, c, s1, s2)
    k_blocks = _rope(k, c, s1, s2)
    blocks = ROW_TILE // BAND
    for hb in range(N_HEAD_BLOCKS):
        natural = (q_blocks[hb] * (ATTN_SCALE * LOG2_E), k_blocks[hb], v[:, hb * LANES:(hb + 1) * LANES])
        for ai, dst in enumerate((qa_ref, ka_ref, va_ref)):
            stage = perm_scr.at[3 * hb + ai]
            stage[...] = natural[ai]
            for r in range(ROW_GROUPS):
                same_r = stage[pl.ds(r, blocks * SUBLANES, stride=ROW_GROUPS), :]
                for bk in range(blocks):
                    dst[hb, pl.ds(bk * BAND + r * SUBLANES, SUBLANES), :] = (
                        same_r[bk * SUBLANES:(bk + 1) * SUBLANES])

    @pl.when(t >= n_t - MAX_WINDOW // ROW_TILE)
    def _():
        kn_ref[...] = jnp.concatenate(k_blocks, axis=1)
        vn_ref[...] = v


def _prompt_in(x, g_mix, w_in, conv_w, conv_b, w_gate, b_gate, lam, tabs):
    b, t, _ = x.shape
    n_t = t // ROW_TILE
    keep_tiles = MAX_WINDOW // ROW_TILE
    whole = pl.BlockSpec(memory_space=pltpu.VMEM)
    tab_spec = pl.BlockSpec((ROW_TILE, LANES), lambda bi, ti: (ti, 0))
    att_spec = pl.BlockSpec((None, N_HEAD_BLOCKS, ROW_TILE, LANES), lambda bi, ti: (bi, 0, ti, 0))
    nat_spec = pl.BlockSpec((None, ROW_TILE, D_ATT),
                            lambda bi, ti: (bi, jnp.maximum(ti - (n_t - keep_tiles), 0), 0))
    att_shape = jax.ShapeDtypeStruct((b, N_HEAD_BLOCKS, t, LANES), F32)
    nat_shape = jax.ShapeDtypeStruct((b, MAX_WINDOW, D_ATT), F32)
    return pl.pallas_call(
        functools.partial(_prompt_in_body, n_t),
        grid=(b, n_t),
        in_specs=[pl.BlockSpec((None, ROW_TILE, D_MODEL), lambda bi, ti: (bi, ti, 0)),
                  whole, whole, whole, whole, whole, whole, whole, tab_spec, tab_spec, tab_spec],
        out_specs=[pl.BlockSpec((None, ROW_TILE, D_LRU), lambda bi, ti: (bi, ti, 0)),
                   att_spec, att_spec, att_spec, nat_spec, nat_spec,
                   pl.BlockSpec((None, SUBLANES, D_LRU), lambda bi, ti: (bi, 0, 0)),
                   pl.BlockSpec((None, 1, D_LRU), lambda bi, ti: (bi, 0, 0))],
        out_shape=[jax.ShapeDtypeStruct((b, t, D_LRU), BF16), att_shape, att_shape, att_shape,
                   nat_shape, nat_shape,
                   jax.ShapeDtypeStruct((b, SUBLANES, D_LRU), F32),
                   jax.ShapeDtypeStruct((b, 1, D_LRU), F32)],
        scratch_shapes=[pltpu.VMEM((SUBLANES, D_LRU), F32), pltpu.VMEM((1, D_LRU), F32),
                        pltpu.VMEM((3 * N_HEAD_BLOCKS, ROW_TILE, LANES), F32)],
        compiler_params=pltpu.CompilerParams(dimension_semantics=("arbitrary", "arbitrary"),
                                             vmem_limit_bytes=VMEM_LIMIT),
        name="prompt_in",
    )(x, g_mix, w_in, conv_w, conv_b, w_gate, b_gate, lam, *tabs)


ROW_GROUPS = BAND // SUBLANES
PIPE_UNITS = 8


def _stored_row(i):
    return (i % ROW_GROUPS) * SUBLANES + i // ROW_GROUPS


def _unit_layout(d):
    t = d * np.arange(BAND)
    stored = BAND * (t // BAND) + _stored_row(t % BAND)
    order = np.argsort(stored)
    stored = np.sort(stored)
    offs = stored[::SUBLANES]
    assert (stored.reshape(ROW_GROUPS, SUBLANES) == offs[:, None] + np.arange(SUBLANES)).all()
    return order, [int(o) for o in offs]


def _unit_bias(order):
    nk = np.arange(2 * BAND)[None, :]
    dist = BAND + order[:, None] - (order[nk % BAND] + BAND * (nk // BAND))
    ok = (dist >= 0) & (dist <= BAND)
    return np.stack([np.where(ok, 0.0, NEG_INF), np.where(ok & (nk >= BAND), 0.0, NEG_INF)])


def _prompt_attn_body(layouts, bias_ref, q_ref, kc_ref, kp_ref, vc_ref, vp_ref, o_ref,
                      pbuf, stage, *accs):
    i = pl.program_id(2)
    lane = lax.broadcasted_iota(jnp.int32, (1, LANES), 1)
    head0 = lane < HEAD_DIM
    heads = (head0, ~head0)
    nt = (((1,), (1,)), ((), ()))

    def groups(ref, base, offs):
        return jnp.concatenate([ref[pl.ds(base + o, SUBLANES), :] for o in offs], axis=0)

    def put_groups(ref, base, offs, val):
        for g, o in enumerate(offs):
            ref[pl.ds(base + o, SUBLANES), :] = val[g * SUBLANES:(g + 1) * SUBLANES]

    def unit_base(pi, un):
        d = DILATIONS[pi]
        sb = un // d
        return sb, sb * (BAND * d) + SUBLANES * (un % d)

    def keys(prev_ref, cur_ref, pi, base):
        offs = layouts[pi][1]
        start = base - BAND * DILATIONS[pi]
        prev = groups(prev_ref, ATT_TILE + start, offs) if start < 0 else groups(cur_ref, start, offs)
        return jnp.concatenate([prev, groups(cur_ref, base, offs)], axis=0).astype(BF16)

    def scores_phase(pi, un, slot):
        sb, base = unit_base(pi, un)
        offs = layouts[pi][1]
        qb = groups(q_ref, base, offs).astype(BF16)
        kb = keys(kp_ref, kc_ref, pi, base)
        mask = bias_ref[2 * pi + jnp.where(i == 0, 1, 0)] if sb == 0 else bias_ref[2 * pi]
        mx = []
        for hh, own in enumerate(heads):
            qh = jnp.where(own, qb, jnp.zeros_like(qb))
            s = lax.dot_general(qh, kb, nt, preferred_element_type=F32) + mask
            m = jnp.max(s, axis=-1, keepdims=True)
            pbuf[2 * slot + hh] = jnp.exp2(s - m).astype(BF16)
            mx.append(m)
        put_groups(accs[3 * pi + 1], base, offs, jnp.where(head0, mx[0], mx[1]))

    def values_phase(pi, un, slot):
        _, base = unit_base(pi, un)
        offs = layouts[pi][1]
        vb = keys(vp_ref, vc_ref, pi, base)
        res = [jnp.dot(pbuf[2 * slot + hh], jnp.where(own, vb, jnp.ones_like(vb)),
                       preferred_element_type=F32) for hh, own in enumerate(heads)]
        put_groups(accs[3 * pi], base, offs, jnp.where(head0, res[0], res[1]))
        put_groups(accs[3 * pi + 2], base, offs,
                   pltpu.roll(jnp.where(head0, res[1], res[0]), HEAD_DIM, axis=1))

    n_groups = ATT_TILE // BAND // PIPE_UNITS
    for j in range(PIPE_UNITS):
        scores_phase(0, j, j)
    for pi in range(len(DILATIONS)):
        for g in range(1, n_groups):
            par = g % 2
            for j in range(PIPE_UNITS):
                values_phase(pi, PIPE_UNITS * (g - 1) + j, PIPE_UNITS * (1 - par) + j)
            for j in range(PIPE_UNITS):
                scores_phase(pi, PIPE_UNITS * g + j, PIPE_UNITS * par + j)
        last = n_groups - 1
        for j in range(PIPE_UNITS):
            values_phase(pi, PIPE_UNITS * last + j, PIPE_UNITS * (last % 2) + j)
        if pi + 1 < len(DILATIONS):
            for j in range(PIPE_UNITS):
                scores_phase(pi + 1, j, j)

    chunk = 256

    def combine(ci, carry):
        rows = pl.ds(pl.multiple_of(ci * chunk, chunk), chunk)
        ms = [accs[3 * pi + 1][rows, :] for pi in range(len(DILATIONS))]
        top = functools.reduce(jnp.maximum, ms)
        ws = [jnp.exp2(m - top) for m in ms]
        num = sum(w * accs[3 * pi][rows, :] for pi, w in enumerate(ws))
        den = sum(w * accs[3 * pi + 2][rows, :] for pi, w in enumerate(ws))
        stage[rows, :] = num / den
        return carry

    lax.fori_loop(0, ATT_TILE // chunk, combine, 0)

    pack = 2 * SUBLANES

    def unpermute(bi, carry):
        blk = pl.multiple_of(bi * BAND, BAND)
        for c in range(BAND // pack):
            o_ref[pl.ds(blk + pack * c, pack), :] = stage[pl.ds(blk + c, pack, stride=SUBLANES), :].astype(BF16)
        return carry

    lax.fori_loop(0, ATT_TILE // BAND, unpermute, 0)


def _prompt_attn(qa, ka, va):
    b, _, t, _ = qa.shape
    n_i = t // ATT_TILE
    assert max(DILATIONS) == ROW_GROUPS and (ATT_TILE // BAND) % PIPE_UNITS == 0
    layouts = [_unit_layout(d) for d in DILATIONS]
    bias = jnp.asarray(np.concatenate([_unit_bias(order) for order, _ in layouts]), F32)
    cur = pl.BlockSpec((None, None, ATT_TILE, LANES), lambda bi, hb, ii: (bi, hb, ii, 0))
    prev = pl.BlockSpec((None, None, ATT_TILE, LANES),
                        lambda bi, hb, ii: (bi, hb, jnp.maximum(ii - 1, 0), 0))
    acc = pltpu.VMEM((ATT_TILE, LANES), F32)
    return pl.pallas_call(
        functools.partial(_prompt_attn_body, layouts),
        grid=(b, N_HEAD_BLOCKS, n_i),
        in_specs=[pl.BlockSpec(memory_space=pltpu.VMEM), cur, cur, prev, cur, prev],
        out_specs=pl.BlockSpec((None, ATT_TILE, LANES), lambda bi, hb, ii: (bi, ii, hb)),
        out_shape=jax.ShapeDtypeStruct((b, t, D_ATT), BF16),
        scratch_shapes=[pltpu.VMEM((2 * PIPE_UNITS * HEADS_PER_BLOCK, BAND, 2 * BAND), BF16),
                        acc] + [acc] * (3 * len(DILATIONS)),
        compiler_params=pltpu.CompilerParams(dimension_semantics=("arbitrary",) * 3,
                                             vmem_limit_bytes=VMEM_LIMIT),
        name="prompt_attn",
    )(bias, qa, ka, ka, va, va)


def _out_ffn(x, ya, yb, wo_ref, gf_ref, wg_ref, wu_ref, wd_ref, cw_ref, cb_ref, gfin_ref, act_scr,
             shifted):
    x1 = (x + jnp.dot(ya, wo_ref[0:D_LRU, :], preferred_element_type=F32)
          + jnp.dot(yb, wo_ref[D_LRU:, :], preferred_element_type=F32))
    h = _rms(x1, gf_ref[...]).astype(BF16)
    for ci in range(D_FF // FF_CHUNK):
        cols = slice(ci * FF_CHUNK, (ci + 1) * FF_CHUNK)
        uf = jnp.dot(h, wg_ref[:, cols], preferred_element_type=F32)
        back1, back2 = shifted(uf, cols)
        uc = (cb_ref[:, cols] + back2 * cw_ref[0:1, cols] + back1 * cw_ref[1:2, cols]
              + uf * cw_ref[2:3, cols])
        up = jnp.dot(h, wu_ref[:, cols], preferred_element_type=F32)
        act_scr[:, cols] = (_gelu(uc) * up).astype(BF16)
    x2 = x1 + jnp.dot(act_scr[...], wd_ref[...], preferred_element_type=F32)
    return _rms(x2, gfin_ref[...])


def _prompt_out_body(x_ref, ya_ref, yb_ref, wo_ref, gf_ref, wg_ref, wu_ref, wd_ref, cw_ref, cb_ref,
                     gfin_ref, y_ref, ftail_ref, tail_scr, act_scr):
    t = pl.program_id(1)

    @pl.when(t == 0)
    def _():
        tail_scr[...] = jnp.zeros_like(tail_scr)

    def shifted(uf, cols):
        tail = tail_scr[:, cols]
        tail_scr[:, cols] = uf[uf.shape[0] - SUBLANES:]
        return _shift_rows(uf, tail, 1), _shift_rows(uf, tail, 2)

    y_ref[...] = _out_ffn(x_ref[...], ya_ref[...], yb_ref[...], wo_ref, gf_ref, wg_ref, wu_ref, wd_ref,
                          cw_ref, cb_ref, gfin_ref, act_scr, shifted)
    ftail_ref[...] = tail_scr[...]


def _prompt_out(x, ya, yb, w_out, g_ffn, w_gate, w_up, w_down, conv_w, conv_b, g_final):
    b, t, _ = x.shape
    whole = pl.BlockSpec(memory_space=pltpu.VMEM)

    def rows(width):
        return pl.BlockSpec((None, ROW_TILE, width), lambda bi, ti: (bi, ti, 0))

    return pl.pallas_call(
        _prompt_out_body,
        grid=(b, t // ROW_TILE),
        in_specs=[rows(D_MODEL), rows(D_LRU), rows(D_ATT)] + [whole] * 8,
        out_specs=[rows(D_MODEL), pl.BlockSpec((None, SUBLANES, D_FF), lambda bi, ti: (bi, 0, 0))],
        out_shape=[jax.ShapeDtypeStruct((b, t, D_MODEL), F32),
                   jax.ShapeDtypeStruct((b, SUBLANES, D_FF), F32)],
        scratch_shapes=[pltpu.VMEM((SUBLANES, D_FF), F32), pltpu.VMEM((ROW_TILE, D_FF), BF16)],
        compiler_params=pltpu.CompilerParams(dimension_semantics=("arbitrary", "arbitrary"),
                                             vmem_limit_bytes=VMEM_LIMIT),
        name="prompt_out",
    )(x, ya, yb, w_out, g_ffn, w_gate, w_up, w_down, conv_w, conv_b, g_final)


def _sample_in_body(n_b, n_s, x_ref, hist_ref, h0_ref, g_ref, win_ref, cw_ref, cb_ref, wgate_ref,
                    bgate_ref, lam_ref, c_ref, s1_ref, s2_ref,
                    ya_ref, q_ref, k_ref, v_ref, cst_ref, hlast_ref):
    z = _bdot(_rms(x_ref[...], g_ref[...]), win_ref[...])
    xa = z[:, :D_LRU]
    ga = z[:, D_LRU:2 * D_LRU]
    q = z[:, 2 * D_LRU:2 * D_LRU + D_ATT]
    k = z[:, 2 * D_LRU + D_ATT:2 * D_LRU + 2 * D_ATT]
    v = z[:, 2 * D_LRU + 2 * D_ATT:]

    ext = jnp.concatenate([hist_ref[...], xa], axis=0)
    xa_c = cb_ref[...] + ext[0:n_s * n_b] * cw_ref[0:1, :]
    for j in range(1, CONV_A):
        xa_c = xa_c + ext[j * n_b:(j + n_s) * n_b] * cw_ref[j:j + 1, :]
    cst_ref[...] = ext[n_s * n_b:]

    a, u = _lru_coeffs(xa_c, wgate_ref, bgate_ref, lam_ref)
    h = h0_ref[...]
    hs = []
    for s in range(n_s):
        h = a[s * n_b:(s + 1) * n_b] * h + u[s * n_b:(s + 1) * n_b]
        hs.append(h)
    hlast_ref[...] = h
    ya_ref[...] = (jnp.concatenate(hs, axis=0) * _gelu(ga)).astype(BF16)

    c, s1, s2 = c_ref[...], s1_ref[...], s2_ref[...]
    q_ref[...] = jnp.concatenate(_rope(q, c, s1, s2), axis=1) * ATTN_SCALE
    k_ref[...] = jnp.concatenate(_rope(k, c, s1, s2), axis=1)
    v_ref[...] = v


def _sample_in(x, hist, h0, g_mix, w_in, conv_w, conv_b, w_gate, b_gate, lam, tabs, n_b, n_s):
    rows = n_b * n_s
    att = jax.ShapeDtypeStruct((rows, D_ATT), F32)
    return pl.pallas_call(
        functools.partial(_sample_in_body, n_b, n_s),
        out_shape=[jax.ShapeDtypeStruct((rows, D_LRU), BF16), att, att, att,
                   jax.ShapeDtypeStruct(((CONV_A - 1) * n_b, D_LRU), F32),
                   jax.ShapeDtypeStruct((n_b, D_LRU), F32)],
        compiler_params=pltpu.CompilerParams(vmem_limit_bytes=VMEM_LIMIT),
        name="sample_in",
    )(x, hist, h0, g_mix, w_in, conv_w, conv_b, w_gate, b_gate, lam, *tabs)


def _sample_attn_body(q_ref, kn_ref, vn_ref, kt_ref, vt_ref, mult_ref, multn_ref, o_ref):
    mult = mult_ref[...]
    multn = multn_ref[...]
    nt = (((1,), (1,)), ((), ()))
    for h in range(N_HEADS):
        q = q_ref[h].astype(BF16)
        s = jnp.dot(q, kt_ref[h].astype(BF16), preferred_element_type=F32)
        s = jnp.where(mult > 0.0, s, NEG_INF)
        sn = lax.dot_general(q, kn_ref[h].astype(BF16), nt, preferred_element_type=F32)
        sn = jnp.where(multn > 0.0, sn, NEG_INF)
        m = jnp.maximum(jnp.max(s, axis=-1, keepdims=True), jnp.max(sn, axis=-1, keepdims=True))
        p = mult * jnp.exp(s - m)
        pn = multn * jnp.exp(sn - m)
        l = jnp.sum(p, axis=-1, keepdims=True) + jnp.sum(pn, axis=-1, keepdims=True)
        o = lax.dot_general(p.astype(BF16), vt_ref[h].astype(BF16), nt, preferred_element_type=F32)
        o = o + jnp.dot(pn.astype(BF16), vn_ref[h].astype(BF16), preferred_element_type=F32)
        o_ref[h] = o / l


def _pattern_counts(n_s, win):
    delta = win + jnp.arange(n_s)[:, None] - jnp.arange(win + n_s)[None, :]
    counts = sum(((delta >= 0) & (delta % d == 0) & (delta <= BAND * d)).astype(F32) for d in DILATIONS)
    return counts[:, :win], counts[:, win:]


def _sample_attn(q, k_new, v_new, cache_kt, cache_vt):
    n_b, _, n_s, _ = q.shape
    win = cache_kt.shape[-1]
    mult, multn = _pattern_counts(n_s, win)
    small = pl.BlockSpec((None, N_HEADS, n_s, HEAD_DIM), lambda bi: (bi, 0, 0, 0))
    cache = pl.BlockSpec((None, N_HEADS, HEAD_DIM, win), lambda bi: (bi, 0, 0, 0))
    whole = pl.BlockSpec(memory_space=pltpu.VMEM)
    return pl.pallas_call(
        _sample_attn_body,
        grid=(n_b,),
        in_specs=[small, small, small, cache, cache, whole, whole],
        out_specs=small,
        out_shape=jax.ShapeDtypeStruct((n_b, N_HEADS, n_s, HEAD_DIM), F32),
        compiler_params=pltpu.CompilerParams(dimension_semantics=("arbitrary",),
                                             vmem_limit_bytes=VMEM_LIMIT),
        name="sample_attn",
    )(q, k_new, v_new, cache_kt, cache_vt, mult, multn)


def _sample_out_body(n_b, x_ref, ya_ref, yb_ref, hist_ref, wo_ref, gf_ref, wg_ref, wu_ref, wd_ref,
                     cw_ref, cb_ref, gfin_ref, y_ref, fst_ref, act_scr):
    n_rows = x_ref.shape[0]

    def shifted(uf, cols):
        ext = jnp.concatenate([hist_ref[:, cols], uf], axis=0)
        fst_ref[:, cols] = ext[n_rows:]
        return ext[n_b:n_b + n_rows], ext[0:n_rows]

    y_ref[...] = _out_ffn(x_ref[...], ya_ref[...], yb_ref[...].astype(BF16), wo_ref, gf_ref, wg_ref,
                          wu_ref, wd_ref, cw_ref, cb_ref, gfin_ref, act_scr, shifted)


def _sample_out(x, ya, yb, hist, w_out, g_ffn, w_gate, w_up, w_down, conv_w, conv_b, g_final, n_b):
    rows = x.shape[0]
    return pl.pallas_call(
        functools.partial(_sample_out_body, n_b),
        out_shape=[jax.ShapeDtypeStruct((rows, D_MODEL), F32),
                   jax.ShapeDtypeStruct(((CONV_F - 1) * n_b, D_FF), F32)],
        scratch_shapes=[pltpu.VMEM((rows, D_FF), BF16)],
        compiler_params=pltpu.CompilerParams(vmem_limit_bytes=VMEM_LIMIT),
        name="sample_out",
    )(x, ya, yb, hist, w_out, g_ffn, w_gate, w_up, w_down, conv_w, conv_b, g_final)


def _rope_tables(pos):
    n = pos.shape[0]
    inv_freq = ROPE_THETA ** (-2.0 * jnp.arange(ROT_HALF, dtype=F32) / ROT_DIM)
    ang = pos[:, None] * inv_freq[None, :]
    cos, sin = jnp.cos(ang), jnp.sin(ang)
    rest = HEAD_DIM - ROT_DIM
    zeros_h = jnp.zeros((n, ROT_HALF), F32)
    c = jnp.concatenate([cos, cos, jnp.ones((n, rest), F32)], axis=1)
    s1 = jnp.concatenate([zeros_h, sin, jnp.zeros((n, rest), F32)], axis=1)
    s2 = jnp.concatenate([-sin, zeros_h, jnp.zeros((n, rest), F32)], axis=1)
    return tuple(jnp.tile(x, (1, HEADS_PER_BLOCK)) for x in (c, s1, s2))


def _block_diag(w):
    nb, n, _ = w.shape
    eye = jnp.eye(nb, dtype=w.dtype)
    return (eye[:, None, :, None] * w[:, :, None, :]).reshape(nb * n, nb * n)


def kernel(x_prompt, x_sample, state_conv_a, state_lru_h, cache_win_k, cache_win_v, state_conv_ffn,
           g_mix, w_in, conv_a_w, conv_a_b, lru_w_r, lru_b_r, lru_w_i, lru_b_i, lru_lambda, w_out,
           g_ffn, w_ffn_gate, conv_f_w, conv_f_b, w_ffn_up, w_ffn_down, g_final):
    n_p, seq_p, _ = x_prompt.shape
    n_s, seq_s, _ = x_sample.shape
    depth = w_in.shape[0]
    win = cache_win_k.shape[2]
    assert depth == 1
    assert seq_p % ATT_TILE == 0 and win == MAX_WINDOW and 2 * seq_s == SUBLANES

    tabs_p = _rope_tables(jnp.arange(seq_p, dtype=F32))
    tabs_s = tuple(jnp.repeat(x, n_s, axis=0)
                   for x in _rope_tables(PAST_LEN + jnp.arange(seq_s, dtype=F32)))
    g_fin = g_final.reshape(1, D_MODEL)

    xp = x_prompt
    xs = x_sample.transpose(1, 0, 2).reshape(seq_s * n_s, D_MODEL)
    outs = [[] for _ in range(10)]
    for layer in range(depth):
        w_in_b = w_in[layer].astype(BF16)
        w_gate = jnp.concatenate([_block_diag(lru_w_r[layer]), _block_diag(lru_w_i[layer])],
                                 axis=1).astype(BF16)
        b_gate = jnp.concatenate([lru_b_r[layer], lru_b_i[layer]]).reshape(1, 2 * D_LRU)
        lam = lru_lambda[layer].reshape(1, D_LRU)
        g_mix_l = g_mix[layer].reshape(1, D_MODEL)
        g_ffn_l = g_ffn[layer].reshape(1, D_MODEL)
        conv_a_b_l = conv_a_b[layer].reshape(1, D_LRU)
        conv_f_b_l = conv_f_b[layer].reshape(1, D_FF)
        w_out_b = w_out[layer].astype(BF16)
        w_fg = w_ffn_gate[layer].astype(BF16)
        w_fu = w_ffn_up[layer].astype(BF16)
        w_fd = w_ffn_down[layer].astype(BF16)

        ya, qa, ka, va, kn, vn, ctail, hlast = _prompt_in(
            xp, g_mix_l, w_in_b, conv_a_w[layer], conv_a_b_l, w_gate, b_gate, lam, tabs_p)
        yb = _prompt_attn(qa, ka, va)
        yp, ftail = _prompt_out(xp, ya, yb, w_out_b, g_ffn_l, w_fg, w_fu, w_fd, conv_f_w[layer],
                                conv_f_b_l, g_fin)
        outs[0].append(ctail[:, SUBLANES - (CONV_A - 1):])
        outs[1].append(hlast[:, 0])
        outs[2].append(kn.reshape(n_p, MAX_WINDOW, N_HEADS, HEAD_DIM))
        outs[3].append(vn.reshape(n_p, MAX_WINDOW, N_HEADS, HEAD_DIM))
        outs[4].append(ftail[:, SUBLANES - (CONV_F - 1):])

        hist_a = state_conv_a[layer].transpose(1, 0, 2).reshape((CONV_A - 1) * n_s, D_LRU)
        hist_f = state_conv_ffn[layer].transpose(1, 0, 2).reshape((CONV_F - 1) * n_s, D_FF)
        ya_s, q_s, k_s, v_s, cst_s, hlast_s = _sample_in(
            xs, hist_a, state_lru_h[layer], g_mix_l, w_in_b, conv_a_w[layer], conv_a_b_l, w_gate,
            b_gate, lam, tabs_s, n_s, seq_s)

        def per_head(x):
            return x.reshape(seq_s, n_s, N_HEADS, HEAD_DIM).transpose(1, 2, 0, 3)

        yb_s = _sample_attn(per_head(q_s), per_head(k_s), per_head(v_s),
                            cache_win_k[layer].transpose(0, 2, 3, 1),
                            cache_win_v[layer].transpose(0, 2, 3, 1))
        yb_s = yb_s.transpose(2, 0, 1, 3).reshape(seq_s * n_s, D_ATT)
        ys, fst_s = _sample_out(xs, ya_s, yb_s, hist_f, w_out_b, g_ffn_l, w_fg, w_fu, w_fd,
                                conv_f_w[layer], conv_f_b_l, g_fin, n_s)
        outs[5].append(cst_s.reshape(CONV_A - 1, n_s, D_LRU).transpose(1, 0, 2))
        outs[6].append(hlast_s)
        outs[7].append(k_s.reshape(seq_s, n_s, N_HEADS, HEAD_DIM).transpose(1, 0, 2, 3))
        outs[8].append(v_s.reshape(seq_s, n_s, N_HEADS, HEAD_DIM).transpose(1, 0, 2, 3))
        outs[9].append(fst_s.reshape(CONV_F - 1, n_s, D_FF).transpose(1, 0, 2))
        xp, xs = yp, ys

    y_sample = xs.reshape(seq_s, n_s, D_MODEL).transpose(1, 0, 2)
    return (xp, y_sample) + tuple(jnp.stack(o, axis=0) for o in outs)
```

```python
import functools

import jax
import jax.numpy as jnp
import numpy as np
from jax import lax
from jax.experimental import pallas as pl
from jax.experimental.pallas import tpu as pltpu

F32 = jnp.float32
BF16 = jnp.bfloat16

D_MODEL = 1024
D_LRU = 512
N_LRU_BLOCKS = 8
LRU_BLOCK = D_LRU // N_LRU_BLOCKS
LRU_C = 8.0
CONV_A = 4
D_ATT = 512
N_HEADS = 8
HEAD_DIM = 64
ROT_DIM = 16
ROT_HALF = ROT_DIM // 2
ROPE_THETA = 500000.0
DILATIONS = (1, 4, 16)
BAND = 128
MAX_WINDOW = 2048
PAST_LEN = 8192
ATTN_SCALE = HEAD_DIM ** -0.5
LOG2_E = 1.4426950408889634
NEG_INF = -1e30
D_IN = 2 * D_LRU + 3 * D_ATT
D_FF = 3 * D_MODEL
CONV_F = 3
EPS = 1e-6

LANES = 128
SUBLANES = 8
HEADS_PER_BLOCK = LANES // HEAD_DIM
N_HEAD_BLOCKS = D_ATT // LANES
ATT_TILE = MAX_WINDOW
ROW_TILE = 512
IN_TILE = 256
FF_CHUNK = 512
VMEM_LIMIT = 56 * 1024 * 1024


def _rms(x, g):
    return x * lax.rsqrt(jnp.mean(x * x, axis=-1, keepdims=True) + EPS) * g


def _gelu(x):
    return 0.5 * x * (1.0 + jnp.tanh(0.7978845608028654 * (x + 0.044715 * (x * x * x))))


def _sigmoid(x):
    return 0.5 + 0.5 * jnp.tanh(0.5 * x)


def _softplus(x):
    return jnp.maximum(x, 0.0) + jnp.log1p(jnp.exp(-jnp.abs(x)))


def _bdot(a, b):
    return jnp.dot(a.astype(BF16), b, preferred_element_type=F32)


def _shift_rows(x, tail, sh):
    rolled = pltpu.roll(x, sh, axis=0)
    rows = lax.broadcasted_iota(jnp.int32, tail.shape, 0)
    head = jnp.where(rows < sh, pltpu.roll(tail, sh, axis=0), rolled[0:SUBLANES])
    return jnp.concatenate([head, rolled[SUBLANES:]], axis=0)


def _rope(x, c, s1, s2):
    out = []
    for hb in range(N_HEAD_BLOCKS):
        blk = x[:, hb * LANES:(hb + 1) * LANES]
        out.append(blk * c + pltpu.roll(blk, ROT_HALF, axis=1) * s1
                   + pltpu.roll(blk, LANES - ROT_HALF, axis=1) * s2)
    return out


def _lru_coeffs(xa_c, wgate_ref, bgate_ref, lam_ref):
    gates = _bdot(xa_c, wgate_ref[...]) + bgate_ref[...]
    r = _sigmoid(gates[:, :D_LRU])
    gi = _sigmoid(gates[:, D_LRU:])
    log_a = -LRU_C * r * _softplus(-lam_ref[...])
    a = jnp.exp(log_a)
    u = jnp.exp2(0.5 * jnp.log2(jnp.tanh(-log_a) * (1.0 + a * a))) * (gi * xa_c)
    return a, u


def _lru_scan(a, u, h0):
    n = a.shape[0]
    g = n // SUBLANES
    a3 = a.reshape(g, SUBLANES, D_LRU)
    u3 = u.reshape(g, SUBLANES, D_LRU)
    rows = lax.broadcasted_iota(jnp.int32, a3.shape, 1)
    sh = 1
    while sh < SUBLANES:
        keep = rows >= sh
        a_s = jnp.where(keep, pltpu.roll(a3, sh, axis=1), 1.0)
        u_s = jnp.where(keep, pltpu.roll(u3, sh, axis=1), 0.0)
        u3 = a3 * u_s + u3
        a3 = a3 * a_s
        sh *= 2
    carry = h0
    hs = []
    for i in range(g):
        hg = u3[i] + a3[i] * carry
        carry = hg[SUBLANES - 1:SUBLANES]
        hs.append(hg)
    return jnp.concatenate(hs, axis=0), carry


def _prompt_in_body(n_t, x_ref, g_ref, win_ref, cw_ref, cb_ref, wgate_ref, bgate_ref, lam_ref,
                    c_ref, s1_ref, s2_ref, sq_ref, skn_ref, svn_ref, skt_ref, svt_ref, mult_ref, multn_ref,
                    ya_ref, qa_ref, ka_ref, va_ref, kn_ref, vn_ref, ctail_ref, hlast_ref, so_ref,
                    tail_scr, h_scr, perm_scr):
    t = pl.program_id(1)

    @pl.when(t == 0)
    def _():
        tail_scr[...] = jnp.zeros_like(tail_scr)
        h_scr[...] = jnp.zeros_like(h_scr)

    _sample_attn_row(sq_ref, skn_ref, svn_ref, skt_ref, svt_ref, mult_ref, multn_ref, so_ref)
    x = x_ref[...]
    z = _bdot(_rms(x, g_ref[...]), win_ref[...])
    xa = z[:, :D_LRU]
    ga = z[:, D_LRU:2 * D_LRU]
    q = z[:, 2 * D_LRU:2 * D_LRU + D_ATT]
    k = z[:, 2 * D_LRU + D_ATT:2 * D_LRU + 2 * D_ATT]
    v = z[:, 2 * D_LRU + 2 * D_ATT:]

    tail = tail_scr[...]
    xa_c = cb_ref[...] + xa * cw_ref[CONV_A - 1:CONV_A, :]
    for j in range(CONV_A - 1):
        xa_c = xa_c + _shift_rows(xa, tail, CONV_A - 1 - j) * cw_ref[j:j + 1, :]
    new_tail = xa[xa.shape[0] - SUBLANES:]
    tail_scr[...] = new_tail
    ctail_ref[...] = new_tail

    a, u = _lru_coeffs(xa_c, wgate_ref, bgate_ref, lam_ref)
    hs, carry = _lru_scan(a, u, h_scr[...])
    h_scr[...] = carry
    hlast_ref[...] = carry
    ya_ref[...] = (hs * _gelu(ga)).astype(BF16)

    c, s1, s2 = c_ref[...], s1_ref[...], s2_ref[...]
    q_blocks = _rope(q, c, s1, s2)
    k_blocks = _rope(k, c, s1, s2)
    blocks = IN_TILE // BAND
    for hb in range(N_HEAD_BLOCKS):
        natural = (q_blocks[hb] * (ATTN_SCALE * LOG2_E), k_blocks[hb], v[:, hb * LANES:(hb + 1) * LANES])
        for ai, dst in enumerate((qa_ref, ka_ref, va_ref)):
            stage = perm_scr.at[3 * hb + ai]
            stage[...] = natural[ai]
            for r in range(ROW_GROUPS):
                same_r = stage[pl.ds(r, blocks * SUBLANES, stride=ROW_GROUPS), :]
                for bk in range(blocks):
                    dst[hb, pl.ds(bk * BAND + r * SUBLANES, SUBLANES), :] = (
                        same_r[bk * SUBLANES:(bk + 1) * SUBLANES])

    @pl.when(t >= n_t - MAX_WINDOW // IN_TILE)
    def _():
        kn_ref[...] = jnp.concatenate(k_blocks, axis=1)
        vn_ref[...] = v


def _prompt_in(x, g_mix, w_in, conv_w, conv_b, w_gate, b_gate, lam, tabs,
               s_q, s_kn, s_vn, cache_kt, cache_vt):
    b, t, _ = x.shape
    n_t = t // IN_TILE
    n_sb, _, n_s, _ = s_q.shape
    win = cache_kt.shape[-1]
    assert b * n_t == n_sb
    mult, multn = _pattern_counts(n_s, win)
    keep_tiles = MAX_WINDOW // IN_TILE
    whole = pl.BlockSpec(memory_space=pltpu.VMEM)
    tab_spec = pl.BlockSpec((IN_TILE, LANES), lambda bi, ti: (ti, 0))
    att_spec = pl.BlockSpec((None, N_HEAD_BLOCKS, IN_TILE, LANES), lambda bi, ti: (bi, 0, ti, 0))
    nat_spec = pl.BlockSpec((None, IN_TILE, D_ATT),
                            lambda bi, ti: (bi, jnp.maximum(ti - (n_t - keep_tiles), 0), 0))
    s_small = pl.BlockSpec((None, N_HEADS, n_s, HEAD_DIM), lambda bi, ti: (bi * n_t + ti, 0, 0, 0))
    s_cache = pl.BlockSpec((None, N_HEADS, HEAD_DIM, win), lambda bi, ti: (bi * n_t + ti, 0, 0, 0))
    att_shape = jax.ShapeDtypeStruct((b, N_HEAD_BLOCKS, t, LANES), F32)
    nat_shape = jax.ShapeDtypeStruct((b, MAX_WINDOW, D_ATT), F32)
    return pl.pallas_call(
        functools.partial(_prompt_in_body, n_t),
        grid=(b, n_t),
        in_specs=[pl.BlockSpec((None, IN_TILE, D_MODEL), lambda bi, ti: (bi, ti, 0)),
                  whole, whole, whole, whole, whole, whole, whole, tab_spec, tab_spec, tab_spec,
                  s_small, s_small, s_small, s_cache, s_cache, whole, whole],
        out_specs=[pl.BlockSpec((None, IN_TILE, D_LRU), lambda bi, ti: (bi, ti, 0)),
                   att_spec, att_spec, att_spec, nat_spec, nat_spec,
                   pl.BlockSpec((None, SUBLANES, D_LRU), lambda bi, ti: (bi, 0, 0)),
                   pl.BlockSpec((None, 1, D_LRU), lambda bi, ti: (bi, 0, 0)),
                   s_small],
        out_shape=[jax.ShapeDtypeStruct((b, t, D_LRU), BF16), att_shape, att_shape, att_shape,
                   nat_shape, nat_shape,
                   jax.ShapeDtypeStruct((b, SUBLANES, D_LRU), F32),
                   jax.ShapeDtypeStruct((b, 1, D_LRU), F32),
                   jax.ShapeDtypeStruct((n_sb, N_HEADS, n_s, HEAD_DIM), F32)],
        scratch_shapes=[pltpu.VMEM((SUBLANES, D_LRU), F32), pltpu.VMEM((1, D_LRU), F32),
                        pltpu.VMEM((3 * N_HEAD_BLOCKS, IN_TILE, LANES), F32)],
        compiler_params=pltpu.CompilerParams(dimension_semantics=("arbitrary", "arbitrary"),
                                             vmem_limit_bytes=VMEM_LIMIT),
        name="prompt_in",
    )(x, g_mix, w_in, conv_w, conv_b, w_gate, b_gate, lam, *tabs,
      s_q, s_kn, s_vn, cache_kt, cache_vt, mult, multn)


ROW_GROUPS = BAND // SUBLANES
PIPE_UNITS = 8


def _stored_row(i):
    return (i % ROW_GROUPS) * SUBLANES + i // ROW_GROUPS


def _unit_layout(d):
    t = d * np.arange(BAND)
    stored = BAND * (t // BAND) + _stored_row(t % BAND)
    order = np.argsort(stored)
    stored = np.sort(stored)
    offs = stored[::SUBLANES]
    assert (stored.reshape(ROW_GROUPS, SUBLANES) == offs[:, None] + np.arange(SUBLANES)).all()
    return order, [int(o) for o in offs]


def _unit_bias(order):
    nk = np.arange(2 * BAND)[None, :]
    dist = BAND + order[:, None] - (order[nk % BAND] + BAND * (nk // BAND))
    ok = (dist >= 0) & (dist <= BAND)
    return np.stack([np.where(ok, 0.0, NEG_INF), np.where(ok & (nk >= BAND), 0.0, NEG_INF)])


def _prompt_attn_body(layouts, bias_ref, q_ref, kc_ref, kp_ref, vc_ref, vp_ref, o_ref,
                      pbuf, stage, *accs):
    i = pl.program_id(2)
    lane = lax.broadcasted_iota(jnp.int32, (1, LANES), 1)
    head0 = lane < HEAD_DIM
    heads = (head0, ~head0)
    nt = (((1,), (1,)), ((), ()))

    def groups(ref, base, offs):
        return jnp.concatenate([ref[pl.ds(base + o, SUBLANES), :] for o in offs], axis=0)

    def put_groups(ref, base, offs, val):
        for g, o in enumerate(offs):
            ref[pl.ds(base + o, SUBLANES), :] = val[g * SUBLANES:(g + 1) * SUBLANES]

    def unit_base(pi, un):
        d = DILATIONS[pi]
        sb = un // d
        return sb, sb * (BAND * d) + SUBLANES * (un % d)

    def keys(prev_ref, cur_ref, pi, base):
        offs = layouts[pi][1]
        start = base - BAND * DILATIONS[pi]
        prev = groups(prev_ref, ATT_TILE + start, offs) if start < 0 else groups(cur_ref, start, offs)
        return jnp.concatenate([prev, groups(cur_ref, base, offs)], axis=0).astype(BF16)

    def scores_phase(pi, un, slot):
        sb, base = unit_base(pi, un)
        offs = layouts[pi][1]
        qb = groups(q_ref, base, offs).astype(BF16)
        kb = keys(kp_ref, kc_ref, pi, base)
        mask = bias_ref[2 * pi + jnp.where(i == 0, 1, 0)] if sb == 0 else bias_ref[2 * pi]
        mx = []
        for hh, own in enumerate(heads):
            qh = jnp.where(own, qb, jnp.zeros_like(qb))
            s = lax.dot_general(qh, kb, nt, preferred_element_type=F32) + mask
            m = jnp.max(s, axis=-1, keepdims=True)
            pbuf[2 * slot + hh] = jnp.exp2(s - m).astype(BF16)
            mx.append(m)
        put_groups(accs[3 * pi + 1], base, offs, jnp.where(head0, mx[0], mx[1]))

    def values_phase(pi, un, slot):
        _, base = unit_base(pi, un)
        offs = layouts[pi][1]
        vb = keys(vp_ref, vc_ref, pi, base)
        res = [jnp.dot(pbuf[2 * slot + hh], jnp.where(own, vb, jnp.ones_like(vb)),
                       preferred_element_type=F32) for hh, own in enumerate(heads)]
        put_groups(accs[3 * pi], base, offs, jnp.where(head0, res[0], res[1]))
        put_groups(accs[3 * pi + 2], base, offs,
                   pltpu.roll(jnp.where(head0, res[1], res[0]), HEAD_DIM, axis=1))

    n_groups = ATT_TILE // BAND // PIPE_UNITS
    for j in range(PIPE_UNITS):
        scores_phase(0, j, j)
    for pi in range(len(DILATIONS)):
        for g in range(1, n_groups):
            par = g % 2
            for j in range(PIPE_UNITS):
                values_phase(pi, PIPE_UNITS * (g - 1) + j, PIPE_UNITS * (1 - par) + j)
            for j in range(PIPE_UNITS):
                scores_phase(pi, PIPE_UNITS * g + j, PIPE_UNITS * par + j)
        last = n_groups - 1
        for j in range(PIPE_UNITS):
            values_phase(pi, PIPE_UNITS * last + j, PIPE_UNITS * (last % 2) + j)
        if pi + 1 < len(DILATIONS):
            for j in range(PIPE_UNITS):
                scores_phase(pi + 1, j, j)

    chunk = 256

    def combine(ci, carry):
        rows = pl.ds(pl.multiple_of(ci * chunk, chunk), chunk)
        ms = [accs[3 * pi + 1][rows, :] for pi in range(len(DILATIONS))]
        top = functools.reduce(jnp.maximum, ms)
        ws = [jnp.exp2(m - top) for m in ms]
        num = sum(w * accs[3 * pi][rows, :] for pi, w in enumerate(ws))
        den = sum(w * accs[3 * pi + 2][rows, :] for pi, w in enumerate(ws))
        stage[rows, :] = num / den
        return carry

    lax.fori_loop(0, ATT_TILE // chunk, combine, 0)

    pack = 2 * SUBLANES

    def unpermute(bi, carry):
        blk = pl.multiple_of(bi * BAND, BAND)
        for c in range(BAND // pack):
            o_ref[pl.ds(blk + pack * c, pack), :] = stage[pl.ds(blk + c, pack, stride=SUBLANES), :].astype(BF16)
        return carry

    lax.fori_loop(0, ATT_TILE // BAND, unpermute, 0)


def _prompt_attn(qa, ka, va):
    b, _, t, _ = qa.shape
    n_i = t // ATT_TILE
    assert max(DILATIONS) == ROW_GROUPS and (ATT_TILE // BAND) % PIPE_UNITS == 0
    layouts = [_unit_layout(d) for d in DILATIONS]
    bias = jnp.asarray(np.concatenate([_unit_bias(order) for order, _ in layouts]), F32)
    cur = pl.BlockSpec((None, None, ATT_TILE, LANES), lambda bi, hb, ii: (bi, hb, ii, 0))
    prev = pl.BlockSpec((None, None, ATT_TILE, LANES),
                        lambda bi, hb, ii: (bi, hb, jnp.maximum(ii - 1, 0), 0))
    acc = pltpu.VMEM((ATT_TILE, LANES), F32)
    return pl.pallas_call(
        functools.partial(_prompt_attn_body, layouts),
        grid=(b, N_HEAD_BLOCKS, n_i),
        in_specs=[pl.BlockSpec(memory_space=pltpu.VMEM), cur, cur, prev, cur, prev],
        out_specs=pl.BlockSpec((None, ATT_TILE, LANES), lambda bi, hb, ii: (bi, ii, hb)),
        out_shape=jax.ShapeDtypeStruct((b, t, D_ATT), BF16),
        scratch_shapes=[pltpu.VMEM((2 * PIPE_UNITS * HEADS_PER_BLOCK, BAND, 2 * BAND), BF16),
                        acc] + [acc] * (3 * len(DILATIONS)),
        compiler_params=pltpu.CompilerParams(dimension_semantics=("arbitrary",) * 3,
                                             vmem_limit_bytes=VMEM_LIMIT),
        name="prompt_attn",
    )(bias, qa, ka, ka, va, va)


def _out_ffn(x, ya, yb, wo_ref, gf_ref, wg_ref, wu_ref, wd_ref, cw_ref, cb_ref, gfin_ref, act_scr,
             shifted):
    x1 = (x + jnp.dot(ya, wo_ref[0:D_LRU, :], preferred_element_type=F32)
          + jnp.dot(yb, wo_ref[D_LRU:, :], preferred_element_type=F32))
    h = _rms(x1, gf_ref[...]).astype(BF16)
    for ci in range(D_FF // FF_CHUNK):
        cols = slice(ci * FF_CHUNK, (ci + 1) * FF_CHUNK)
        uf = jnp.dot(h, wg_ref[:, cols], preferred_element_type=F32)
        back1, back2 = shifted(uf, cols)
        uc = (cb_ref[:, cols] + back2 * cw_ref[0:1, cols] + back1 * cw_ref[1:2, cols]
              + uf * cw_ref[2:3, cols])
        up = jnp.dot(h, wu_ref[:, cols], preferred_element_type=F32)
        act_scr[:, cols] = (_gelu(uc) * up).astype(BF16)
    x2 = x1 + jnp.dot(act_scr[...], wd_ref[...], preferred_element_type=F32)
    return _rms(x2, gfin_ref[...])


def _prompt_out_body(x_ref, ya_ref, yb_ref, wo_ref, gf_ref, wg_ref, wu_ref, wd_ref, cw_ref, cb_ref,
                     gfin_ref, y_ref, ftail_ref, tail_scr, act_scr):
    t = pl.program_id(1)

    @pl.when(t == 0)
    def _():
        tail_scr[...] = jnp.zeros_like(tail_scr)

    def shifted(uf, cols):
        tail = tail_scr[:, cols]
        tail_scr[:, cols] = uf[uf.shape[0] - SUBLANES:]
        return _shift_rows(uf, tail, 1), _shift_rows(uf, tail, 2)

    y_ref[...] = _out_ffn(x_ref[...], ya_ref[...], yb_ref[...], wo_ref, gf_ref, wg_ref, wu_ref, wd_ref,
                          cw_ref, cb_ref, gfin_ref, act_scr, shifted)
    ftail_ref[...] = tail_scr[...]


def _prompt_out(x, ya, yb, w_out, g_ffn, w_gate, w_up, w_down, conv_w, conv_b, g_final):
    b, t, _ = x.shape
    whole = pl.BlockSpec(memory_space=pltpu.VMEM)

    def rows(width):
        return pl.BlockSpec((None, ROW_TILE, width), lambda bi, ti: (bi, ti, 0))

    return pl.pallas_call(
        _prompt_out_body,
        grid=(b, t // ROW_TILE),
        in_specs=[rows(D_MODEL), rows(D_LRU), rows(D_ATT)] + [whole] * 8,
        out_specs=[rows(D_MODEL), pl.BlockSpec((None, SUBLANES, D_FF), lambda bi, ti: (bi, 0, 0))],
        out_shape=[jax.ShapeDtypeStruct((b, t, D_MODEL), F32),
                   jax.ShapeDtypeStruct((b, SUBLANES, D_FF), F32)],
        scratch_shapes=[pltpu.VMEM((SUBLANES, D_FF), F32), pltpu.VMEM((ROW_TILE, D_FF), BF16)],
        compiler_params=pltpu.CompilerParams(dimension_semantics=("arbitrary", "arbitrary"),
                                             vmem_limit_bytes=VMEM_LIMIT),
        name="prompt_out",
    )(x, ya, yb, w_out, g_ffn, w_gate, w_up, w_down, conv_w, conv_b, g_final)


def _sample_in_body(n_b, n_s, x_ref, hist_ref, h0_ref, g_ref, win_ref, cw_ref, cb_ref, wgate_ref,
                    bgate_ref, lam_ref, c_ref, s1_ref, s2_ref,
                    ya_ref, q_ref, k_ref, v_ref, cst_ref, hlast_ref):
    z = _bdot(_rms(x_ref[...], g_ref[...]), win_ref[...])
    xa = z[:, :D_LRU]
    ga = z[:, D_LRU:2 * D_LRU]
    q = z[:, 2 * D_LRU:2 * D_LRU + D_ATT]
    k = z[:, 2 * D_LRU + D_ATT:2 * D_LRU + 2 * D_ATT]
    v = z[:, 2 * D_LRU + 2 * D_ATT:]

    ext = jnp.concatenate([hist_ref[...], xa], axis=0)
    xa_c = cb_ref[...] + ext[0:n_s * n_b] * cw_ref[0:1, :]
    for j in range(1, CONV_A):
        xa_c = xa_c + ext[j * n_b:(j + n_s) * n_b] * cw_ref[j:j + 1, :]
    cst_ref[...] = ext[n_s * n_b:]

    a, u = _lru_coeffs(xa_c, wgate_ref, bgate_ref, lam_ref)
    h = h0_ref[...]
    hs = []
    for s in range(n_s):
        h = a[s * n_b:(s + 1) * n_b] * h + u[s * n_b:(s + 1) * n_b]
        hs.append(h)
    hlast_ref[...] = h
    ya_ref[...] = (jnp.concatenate(hs, axis=0) * _gelu(ga)).astype(BF16)

    c, s1, s2 = c_ref[...], s1_ref[...], s2_ref[...]
    q_ref[...] = jnp.concatenate(_rope(q, c, s1, s2), axis=1) * ATTN_SCALE
    k_ref[...] = jnp.concatenate(_rope(k, c, s1, s2), axis=1)
    v_ref[...] = v


def _sample_in(x, hist, h0, g_mix, w_in, conv_w, conv_b, w_gate, b_gate, lam, tabs, n_b, n_s):
    rows = n_b * n_s
    att = jax.ShapeDtypeStruct((rows, D_ATT), F32)
    return pl.pallas_call(
        functools.partial(_sample_in_body, n_b, n_s),
        out_shape=[jax.ShapeDtypeStruct((rows, D_LRU), BF16), att, att, att,
                   jax.ShapeDtypeStruct(((CONV_A - 1) * n_b, D_LRU), F32),
                   jax.ShapeDtypeStruct((n_b, D_LRU), F32)],
        compiler_params=pltpu.CompilerParams(vmem_limit_bytes=VMEM_LIMIT),
        name="sample_in",
    )(x, hist, h0, g_mix, w_in, conv_w, conv_b, w_gate, b_gate, lam, *tabs)


def _sample_attn_row(q_ref, kn_ref, vn_ref, kt_ref, vt_ref, mult_ref, multn_ref, o_ref):
    mult = mult_ref[...]
    multn = multn_ref[...]
    nt = (((1,), (1,)), ((), ()))
    scores = []
    for h in range(N_HEADS):
        q = q_ref[h].astype(BF16)
        s = jnp.dot(q, kt_ref[h].astype(BF16), preferred_element_type=F32)
        sn = lax.dot_general(q, kn_ref[h].astype(BF16), nt, preferred_element_type=F32)
        scores.append((s, sn))
    probs = []
    for s, sn in scores:
        s = jnp.where(mult > 0.0, s, NEG_INF)
        sn = jnp.where(multn > 0.0, sn, NEG_INF)
        m = jnp.maximum(jnp.max(s, axis=-1, keepdims=True), jnp.max(sn, axis=-1, keepdims=True))
        p = mult * jnp.exp(s - m)
        pn = multn * jnp.exp(sn - m)
        l = jnp.sum(p, axis=-1, keepdims=True) + jnp.sum(pn, axis=-1, keepdims=True)
        probs.append((p.astype(BF16), pn.astype(BF16), l))
    for h, (p, pn, l) in enumerate(probs):
        o = lax.dot_general(p, vt_ref[h].astype(BF16), nt, preferred_element_type=F32)
        o = o + jnp.dot(pn, vn_ref[h].astype(BF16), preferred_element_type=F32)
        o_ref[h] = o / l


def _pattern_counts(n_s, win):
    delta = win + jnp.arange(n_s)[:, None] - jnp.arange(win + n_s)[None, :]
    counts = sum(((delta >= 0) & (delta % d == 0) & (delta <= BAND * d)).astype(F32) for d in DILATIONS)
    return counts[:, :win], counts[:, win:]


def _sample_out_body(n_b, x_ref, ya_ref, yb_ref, hist_ref, wo_ref, gf_ref, wg_ref, wu_ref, wd_ref,
                     cw_ref, cb_ref, gfin_ref, y_ref, fst_ref, act_scr):
    n_rows = x_ref.shape[0]

    def shifted(uf, cols):
        ext = jnp.concatenate([hist_ref[:, cols], uf], axis=0)
        fst_ref[:, cols] = ext[n_rows:]
        return ext[n_b:n_b + n_rows], ext[0:n_rows]

    y_ref[...] = _out_ffn(x_ref[...], ya_ref[...], yb_ref[...].astype(BF16), wo_ref, gf_ref, wg_ref,
                          wu_ref, wd_ref, cw_ref, cb_ref, gfin_ref, act_scr, shifted)


def _sample_out(x, ya, yb, hist, w_out, g_ffn, w_gate, w_up, w_down, conv_w, conv_b, g_final, n_b):
    rows = x.shape[0]
    return pl.pallas_call(
        functools.partial(_sample_out_body, n_b),
        out_shape=[jax.ShapeDtypeStruct((rows, D_MODEL), F32),
                   jax.ShapeDtypeStruct(((CONV_F - 1) * n_b, D_FF), F32)],
        scratch_shapes=[pltpu.VMEM((rows, D_FF), BF16)],
        compiler_params=pltpu.CompilerParams(vmem_limit_bytes=VMEM_LIMIT),
        name="sample_out",
    )(x, ya, yb, hist, w_out, g_ffn, w_gate, w_up, w_down, conv_w, conv_b, g_final)


def _rope_tables(pos):
    n = pos.shape[0]
    inv_freq = ROPE_THETA ** (-2.0 * jnp.arange(ROT_HALF, dtype=F32) / ROT_DIM)
    ang = pos[:, None] * inv_freq[None, :]
    cos, sin = jnp.cos(ang), jnp.sin(ang)
    rest = HEAD_DIM - ROT_DIM
    zeros_h = jnp.zeros((n, ROT_HALF), F32)
    c = jnp.concatenate([cos, cos, jnp.ones((n, rest), F32)], axis=1)
    s1 = jnp.concatenate([zeros_h, sin, jnp.zeros((n, rest), F32)], axis=1)
    s2 = jnp.concatenate([-sin, zeros_h, jnp.zeros((n, rest), F32)], axis=1)
    return tuple(jnp.tile(x, (1, HEADS_PER_BLOCK)) for x in (c, s1, s2))


def _block_diag(w):
    nb, n, _ = w.shape
    eye = jnp.eye(nb, dtype=w.dtype)
    return (eye[:, None, :, None] * w[:, :, None, :]).reshape(nb * n, nb * n)


def kernel(x_prompt, x_sample, state_conv_a, state_lru_h, cache_win_k, cache_win_v, state_conv_ffn,
           g_mix, w_in, conv_a_w, conv_a_b, lru_w_r, lru_b_r, lru_w_i, lru_b_i, lru_lambda, w_out,
           g_ffn, w_ffn_gate, conv_f_w, conv_f_b, w_ffn_up, w_ffn_down, g_final):
    n_p, seq_p, _ = x_prompt.shape
    n_s, seq_s, _ = x_sample.shape
    depth = w_in.shape[0]
    win = cache_win_k.shape[2]
    assert depth == 1
    assert seq_p % ATT_TILE == 0 and win == MAX_WINDOW and 2 * seq_s == SUBLANES

    tabs_p = _rope_tables(jnp.arange(seq_p, dtype=F32))
    tabs_s = tuple(jnp.repeat(x, n_s, axis=0)
                   for x in _rope_tables(PAST_LEN + jnp.arange(seq_s, dtype=F32)))
    g_fin = g_final.reshape(1, D_MODEL)

    xp = x_prompt
    xs = x_sample.transpose(1, 0, 2).reshape(seq_s * n_s, D_MODEL)
    outs = [[] for _ in range(10)]
    for layer in range(depth):
        w_in_b = w_in[layer].astype(BF16)
        w_gate = jnp.concatenate([_block_diag(lru_w_r[layer]), _block_diag(lru_w_i[layer])],
                                 axis=1).astype(BF16)
        b_gate = jnp.concatenate([lru_b_r[layer], lru_b_i[layer]]).reshape(1, 2 * D_LRU)
        lam = lru_lambda[layer].reshape(1, D_LRU)
        g_mix_l = g_mix[layer].reshape(1, D_MODEL)
        g_ffn_l = g_ffn[layer].reshape(1, D_MODEL)
        conv_a_b_l = conv_a_b[layer].reshape(1, D_LRU)
        conv_f_b_l = conv_f_b[layer].reshape(1, D_FF)
        w_out_b = w_out[layer].astype(BF16)
        w_fg = w_ffn_gate[layer].astype(BF16)
        w_fu = w_ffn_up[layer].astype(BF16)
        w_fd = w_ffn_down[layer].astype(BF16)

        hist_a = state_conv_a[layer].transpose(1, 0, 2).reshape((CONV_A - 1) * n_s, D_LRU)
        hist_f = state_conv_ffn[layer].transpose(1, 0, 2).reshape((CONV_F - 1) * n_s, D_FF)
        ya_s, q_s, k_s, v_s, cst_s, hlast_s = _sample_in(
            xs, hist_a, state_lru_h[layer], g_mix_l, w_in_b, conv_a_w[layer], conv_a_b_l, w_gate,
            b_gate, lam, tabs_s, n_s, seq_s)

        def per_head(x):
            return x.reshape(seq_s, n_s, N_HEADS, HEAD_DIM).transpose(1, 2, 0, 3)

        ya, qa, ka, va, kn, vn, ctail, hlast, yb_s = _prompt_in(
            xp, g_mix_l, w_in_b, conv_a_w[layer], conv_a_b_l, w_gate, b_gate, lam, tabs_p,
            per_head(q_s), per_head(k_s), per_head(v_s),
            cache_win_k[layer].transpose(0, 2, 3, 1), cache_win_v[layer].transpose(0, 2, 3, 1))
        yb = _prompt_attn(qa, ka, va)
        yp, ftail = _prompt_out(xp, ya, yb, w_out_b, g_ffn_l, w_fg, w_fu, w_fd, conv_f_w[layer],
                                conv_f_b_l, g_fin)
        outs[0].append(ctail[:, SUBLANES - (CONV_A - 1):])
        outs[1].append(hlast[:, 0])
        outs[2].append(kn.reshape(n_p, MAX_WINDOW, N_HEADS, HEAD_DIM))
        outs[3].append(vn.reshape(n_p, MAX_WINDOW, N_HEADS, HEAD_DIM))
        outs[4].append(ftail[:, SUBLANES - (CONV_F - 1):])

        yb_s = yb_s.transpose(2, 0, 1, 3).reshape(seq_s * n_s, D_ATT)
        ys, fst_s = _sample_out(xs, ya_s, yb_s, hist_f, w_out_b, g_ffn_l, w_fg, w_fu, w_fd,
                                conv_f_w[layer], conv_f_b_l, g_fin, n_s)
        outs[5].append(cst_s.reshape(CONV_A - 1, n_s, D_LRU).transpose(1, 0, 2))
        outs[6].append(hlast_s)
        outs[7].append(k_s.reshape(seq_s, n_s, N_HEADS, HEAD_DIM).transpose(1, 0, 2, 3))
        outs[8].append(v_s.reshape(seq_s, n_s, N_HEADS, HEAD_DIM).transpose(1, 0, 2, 3))
        outs[9].append(fst_s.reshape(CONV_F - 1, n_s, D_FF).transpose(1, 0, 2))
        xp, xs = yp, ys

    y_sample = xs.reshape(seq_s, n_s, D_MODEL).transpose(1, 0, 2)
    return (xp, y_sample) + tuple(jnp.stack(o, axis=0) for o in outs)
```

```python
import functools

import jax
import jax.numpy as jnp
import numpy as np
from jax import lax
from jax.experimental import pallas as pl
from jax.experimental.pallas import tpu as pltpu

F32 = jnp.float32
BF16 = jnp.bfloat16

D_MODEL = 1024
D_LRU = 512
N_LRU_BLOCKS = 8
LRU_BLOCK = D_LRU // N_LRU_BLOCKS
LRU_C = 8.0
CONV_A = 4
D_ATT = 512
N_HEADS = 8
HEAD_DIM = 64
ROT_DIM = 16
ROT_HALF = ROT_DIM // 2
ROPE_THETA = 500000.0
DILATIONS = (1, 4, 16)
BAND = 128
MAX_WINDOW = 2048
PAST_LEN = 8192
ATTN_SCALE = HEAD_DIM ** -0.5
LOG2_E = 1.4426950408889634
NEG_INF = -1e30
D_IN = 2 * D_LRU + 3 * D_ATT
D_FF = 3 * D_MODEL
CONV_F = 3
EPS = 1e-6

LANES = 128
SUBLANES = 8
HEADS_PER_BLOCK = LANES // HEAD_DIM
N_HEAD_BLOCKS = D_ATT // LANES
ATT_TILE = MAX_WINDOW
ROW_TILE = 512
IN_TILE = 256
FF_CHUNK = 512
VMEM_LIMIT = 56 * 1024 * 1024


def _rms(x, g):
    return x * lax.rsqrt(jnp.mean(x * x, axis=-1, keepdims=True) + EPS) * g


def _gelu(x):
    return 0.5 * x * (1.0 + jnp.tanh(0.7978845608028654 * (x + 0.044715 * (x * x * x))))


def _sigmoid(x):
    return 0.5 + 0.5 * jnp.tanh(0.5 * x)


def _softplus(x):
    return jnp.maximum(x, 0.0) + jnp.log1p(jnp.exp(-jnp.abs(x)))


def _bdot(a, b):
    return jnp.dot(a.astype(BF16), b, preferred_element_type=F32)


def _shift_rows(x, tail, sh):
    rolled = pltpu.roll(x, sh, axis=0)
    rows = lax.broadcasted_iota(jnp.int32, tail.shape, 0)
    head = jnp.where(rows < sh, pltpu.roll(tail, sh, axis=0), rolled[0:SUBLANES])
    return jnp.concatenate([head, rolled[SUBLANES:]], axis=0)


def _rope(x, c, s1, s2):
    out = []
    for hb in range(N_HEAD_BLOCKS):
        blk = x[:, hb * LANES:(hb + 1) * LANES]
        out.append(blk * c + pltpu.roll(blk, ROT_HALF, axis=1) * s1
                   + pltpu.roll(blk, LANES - ROT_HALF, axis=1) * s2)
    return out


def _lru_coeffs(xa_c, wgate_ref, bgate_ref, lam_ref):
    gates = _bdot(xa_c, wgate_ref[...]) + bgate_ref[...]
    r = _sigmoid(gates[:, :D_LRU])
    gi = _sigmoid(gates[:, D_LRU:])
    log_a = -LRU_C * r * _softplus(-lam_ref[...])
    a = jnp.exp(log_a)
    u = jnp.exp2(0.5 * jnp.log2(jnp.tanh(-log_a) * (1.0 + a * a))) * (gi * xa_c)
    return a, u


def _lru_scan(a, u, h0):
    n = a.shape[0]
    g = n // SUBLANES
    a3 = a.reshape(g, SUBLANES, D_LRU)
    u3 = u.reshape(g, SUBLANES, D_LRU)
    rows = lax.broadcasted_iota(jnp.int32, a3.shape, 1)
    sh = 1
    while sh < SUBLANES:
        keep = rows >= sh
        a_s = jnp.where(keep, pltpu.roll(a3, sh, axis=1), 1.0)
        u_s = jnp.where(keep, pltpu.roll(u3, sh, axis=1), 0.0)
        u3 = a3 * u_s + u3
        a3 = a3 * a_s
        sh *= 2
    carry = h0
    hs = []
    for i in range(g):
        hg = u3[i] + a3[i] * carry
        carry = hg[SUBLANES - 1:SUBLANES]
        hs.append(hg)
    return jnp.concatenate(hs, axis=0), carry


def _prompt_in_body(n_t, x_ref, g_ref, win_ref, cw_ref, cb_ref, wgate_ref, bgate_ref, lam_ref,
                    c_ref, s1_ref, s2_ref, perm_ref,
                    sq_ref, skn_ref, svn_ref, skt_ref, svt_ref, mult_ref, multn_ref,
                    ya_ref, qa_ref, ka_ref, va_ref, kn_ref, vn_ref, ctail_ref, hlast_ref, so_ref,
                    tail_scr, h_scr):
    t = pl.program_id(1)

    @pl.when(t == 0)
    def _():
        tail_scr[...] = jnp.zeros_like(tail_scr)
        h_scr[...] = jnp.zeros_like(h_scr)

    z = _bdot(_rms(x_ref[...], g_ref[...]), win_ref[...])
    xa = z[:, 0:D_LRU]
    ga = z[:, D_LRU:2 * D_LRU]
    q = z[:, 2 * D_LRU:2 * D_LRU + D_ATT]
    k = z[:, 2 * D_LRU + D_ATT:2 * D_LRU + 2 * D_ATT]
    v = z[:, 2 * D_LRU + 2 * D_ATT:]

    tail = tail_scr[...]
    xa_c = cb_ref[...] + xa * cw_ref[CONV_A - 1:CONV_A, :]
    for j in range(CONV_A - 1):
        xa_c = xa_c + _shift_rows(xa, tail, CONV_A - 1 - j) * cw_ref[j:j + 1, :]
    new_tail = xa[xa.shape[0] - SUBLANES:]
    tail_scr[...] = new_tail
    ctail_ref[...] = new_tail

    a, u = _lru_coeffs(xa_c, wgate_ref, bgate_ref, lam_ref)
    hs, carry = _lru_scan(a, u, h_scr[...])
    h_scr[...] = carry
    hlast_ref[...] = carry
    ya_ref[...] = (hs * _gelu(ga)).astype(BF16)

    heads = range(N_HEADS)
    s_probs = _sample_probs(_sample_scores(heads, sq_ref, skn_ref, skt_ref), mult_ref, multn_ref)
    _sample_values(heads, s_probs, svn_ref, svt_ref, so_ref)

    c, s1, s2 = c_ref[...], s1_ref[...], s2_ref[...]
    q_blocks = _rope(q, c, s1, s2)
    k_blocks = _rope(k, c, s1, s2)
    q_scaled = [blk * (ATTN_SCALE * LOG2_E) for blk in q_blocks]
    qkv = jnp.concatenate(q_scaled + k_blocks + [v], axis=1).astype(BF16)
    shuffle = perm_ref[...]
    for bk in range(IN_TILE // BAND):
        rows = slice(bk * BAND, (bk + 1) * BAND)
        stored = jnp.dot(shuffle, qkv[rows], preferred_element_type=F32)
        for ai, dst in enumerate((qa_ref, ka_ref, va_ref)):
            for hb in range(N_HEAD_BLOCKS):
                lo = ai * D_ATT + hb * LANES
                dst[hb, rows, :] = stored[:, lo:lo + LANES]

    @pl.when(t >= n_t - MAX_WINDOW // IN_TILE)
    def _():
        kn_ref[...] = jnp.concatenate(k_blocks, axis=1)
        vn_ref[...] = v


def _prompt_in(x, g_mix, w_in, conv_w, conv_b, w_gate, b_gate, lam, tabs,
               s_q, s_kn, s_vn, cache_kt, cache_vt):
    b, t, _ = x.shape
    n_t = t // IN_TILE
    n_sb, _, n_s, _ = s_q.shape
    win = cache_kt.shape[-1]
    assert b * n_t == n_sb
    mult, multn = _pattern_counts(n_s, win)
    natural = np.arange(BAND)
    shuffle = jnp.asarray(natural[:, None] == _stored_row(natural)[None, :], BF16)
    keep_tiles = MAX_WINDOW // IN_TILE
    whole = pl.BlockSpec(memory_space=pltpu.VMEM)
    tab_spec = pl.BlockSpec((IN_TILE, LANES), lambda bi, ti: (ti, 0))
    att_spec = pl.BlockSpec((None, N_HEAD_BLOCKS, IN_TILE, LANES), lambda bi, ti: (bi, 0, ti, 0))
    nat_spec = pl.BlockSpec((None, IN_TILE, D_ATT),
                            lambda bi, ti: (bi, jnp.maximum(ti - (n_t - keep_tiles), 0), 0))
    s_small = pl.BlockSpec((None, N_HEADS, n_s, HEAD_DIM), lambda bi, ti: (bi * n_t + ti, 0, 0, 0))
    s_cache = pl.BlockSpec((None, N_HEADS, HEAD_DIM, win), lambda bi, ti: (bi * n_t + ti, 0, 0, 0))
    att_shape = jax.ShapeDtypeStruct((b, N_HEAD_BLOCKS, t, LANES), F32)
    nat_shape = jax.ShapeDtypeStruct((b, MAX_WINDOW, D_ATT), F32)
    return pl.pallas_call(
        functools.partial(_prompt_in_body, n_t),
        grid=(b, n_t),
        in_specs=[pl.BlockSpec((None, IN_TILE, D_MODEL), lambda bi, ti: (bi, ti, 0)),
                  whole, whole, whole, whole, whole, whole, whole, tab_spec, tab_spec, tab_spec,
                  whole, s_small, s_small, s_small, s_cache, s_cache, whole, whole],
        out_specs=[pl.BlockSpec((None, IN_TILE, D_LRU), lambda bi, ti: (bi, ti, 0)),
                   att_spec, att_spec, att_spec, nat_spec, nat_spec,
                   pl.BlockSpec((None, SUBLANES, D_LRU), lambda bi, ti: (bi, 0, 0)),
                   pl.BlockSpec((None, 1, D_LRU), lambda bi, ti: (bi, 0, 0)),
                   s_small],
        out_shape=[jax.ShapeDtypeStruct((b, t, D_LRU), BF16), att_shape, att_shape, att_shape,
                   nat_shape, nat_shape,
                   jax.ShapeDtypeStruct((b, SUBLANES, D_LRU), F32),
                   jax.ShapeDtypeStruct((b, 1, D_LRU), F32),
                   jax.ShapeDtypeStruct((n_sb, N_HEADS, n_s, HEAD_DIM), F32)],
        scratch_shapes=[pltpu.VMEM((SUBLANES, D_LRU), F32), pltpu.VMEM((1, D_LRU), F32)],
        compiler_params=pltpu.CompilerParams(dimension_semantics=("arbitrary", "arbitrary"),
                                             vmem_limit_bytes=VMEM_LIMIT),
        name="prompt_in",
    )(x, g_mix, w_in, conv_w, conv_b, w_gate, b_gate, lam, *tabs, shuffle,
      s_q, s_kn, s_vn, cache_kt, cache_vt, mult, multn)


ROW_GROUPS = BAND // SUBLANES
PIPE_UNITS = 8


def _stored_row(i):
    return (i % ROW_GROUPS) * SUBLANES + i // ROW_GROUPS


def _unit_layout(d):
    t = d * np.arange(BAND)
    stored = BAND * (t // BAND) + _stored_row(t % BAND)
    order = np.argsort(stored)
    stored = np.sort(stored)
    offs = stored[::SUBLANES]
    assert (stored.reshape(ROW_GROUPS, SUBLANES) == offs[:, None] + np.arange(SUBLANES)).all()
    return order, [int(o) for o in offs]


def _unit_bias(order):
    nk = np.arange(2 * BAND)[None, :]
    dist = BAND + order[:, None] - (order[nk % BAND] + BAND * (nk // BAND))
    ok = (dist >= 0) & (dist <= BAND)
    return np.stack([np.where(ok, 0.0, NEG_INF), np.where(ok & (nk >= BAND), 0.0, NEG_INF)])


def _prompt_attn_body(layouts, bias_ref, q_ref, kc_ref, kp_ref, vc_ref, vp_ref, o_ref,
                      pbuf, stage, *accs):
    i = pl.program_id(2)
    lane = lax.broadcasted_iota(jnp.int32, (1, LANES), 1)
    head0 = lane < HEAD_DIM
    heads = (head0, ~head0)
    nt = (((1,), (1,)), ((), ()))

    def groups(ref, base, offs):
        return jnp.concatenate([ref[pl.ds(base + o, SUBLANES), :] for o in offs], axis=0)

    def put_groups(ref, base, offs, val):
        for g, o in enumerate(offs):
            ref[pl.ds(base + o, SUBLANES), :] = val[g * SUBLANES:(g + 1) * SUBLANES]

    def unit_base(pi, un):
        d = DILATIONS[pi]
        sb = un // d
        return sb, sb * (BAND * d) + SUBLANES * (un % d)

    def keys(prev_ref, cur_ref, pi, base):
        offs = layouts[pi][1]
        start = base - BAND * DILATIONS[pi]
        prev = groups(prev_ref, ATT_TILE + start, offs) if start < 0 else groups(cur_ref, start, offs)
        return jnp.concatenate([prev, groups(cur_ref, base, offs)], axis=0).astype(BF16)

    def scores_phase(pi, un, slot):
        sb, base = unit_base(pi, un)
        offs = layouts[pi][1]
        qb = groups(q_ref, base, offs).astype(BF16)
        kb = keys(kp_ref, kc_ref, pi, base)
        mask = bias_ref[2 * pi + jnp.where(i == 0, 1, 0)] if sb == 0 else bias_ref[2 * pi]
        mx = []
        for hh, own in enumerate(heads):
            qh = jnp.where(own, qb, jnp.zeros_like(qb))
            s = lax.dot_general(qh, kb, nt, preferred_element_type=F32) + mask
            m = jnp.max(s, axis=-1, keepdims=True)
            pbuf[2 * slot + hh] = jnp.exp2(s - m).astype(BF16)
            mx.append(m)
        put_groups(accs[3 * pi + 1], base, offs, jnp.where(head0, mx[0], mx[1]))

    def values_phase(pi, un, slot):
        _, base = unit_base(pi, un)
        offs = layouts[pi][1]
        vb = keys(vp_ref, vc_ref, pi, base)
        res = [jnp.dot(pbuf[2 * slot + hh], jnp.where(own, vb, jnp.ones_like(vb)),
                       preferred_element_type=F32) for hh, own in enumerate(heads)]
        o_un = jnp.where(head0, res[0], res[1])
        l_un = pltpu.roll(jnp.where(head0, res[1], res[0]), HEAD_DIM, axis=1)
        if pi + 1 < len(DILATIONS):
            put_groups(accs[3 * pi], base, offs, o_un)
            put_groups(accs[3 * pi + 2], base, offs, l_un)
            return
        os_ = [groups(accs[3 * pj], base, offs) for pj in range(pi)] + [o_un]
        ms = [groups(accs[3 * pj + 1], base, offs) for pj in range(pi + 1)]
        ls = [groups(accs[3 * pj + 2], base, offs) for pj in range(pi)] + [l_un]
        top = functools.reduce(jnp.maximum, ms)
        ws = [jnp.exp2(m - top) for m in ms]
        num = sum(w * o for w, o in zip(ws, os_))
        den = sum(w * l for w, l in zip(ws, ls))
        put_groups(stage, base, offs, num / den)

    n_groups = ATT_TILE // BAND // PIPE_UNITS
    for j in range(PIPE_UNITS):
        scores_phase(0, j, j)
    for pi in range(len(DILATIONS)):
        for g in range(1, n_groups):
            par = g % 2
            for j in range(PIPE_UNITS):
                values_phase(pi, PIPE_UNITS * (g - 1) + j, PIPE_UNITS * (1 - par) + j)
            for j in range(PIPE_UNITS):
                scores_phase(pi, PIPE_UNITS * g + j, PIPE_UNITS * par + j)
        last = n_groups - 1
        for j in range(PIPE_UNITS):
            values_phase(pi, PIPE_UNITS * last + j, PIPE_UNITS * (last % 2) + j)
        if pi + 1 < len(DILATIONS):
            for j in range(PIPE_UNITS):
                scores_phase(pi + 1, j, j)

    pack = 2 * SUBLANES

    def unpermute(bi, carry):
        blk = pl.multiple_of(bi * BAND, BAND)
        for c in range(BAND // pack):
            o_ref[pl.ds(blk + pack * c, pack), :] = stage[pl.ds(blk + c, pack, stride=SUBLANES), :].astype(BF16)
        return carry

    lax.fori_loop(0, ATT_TILE // BAND, unpermute, 0)


def _prompt_attn(qa, ka, va):
    b, _, t, _ = qa.shape
    n_i = t // ATT_TILE
    assert max(DILATIONS) == ROW_GROUPS and (ATT_TILE // BAND) % PIPE_UNITS == 0
    layouts = [_unit_layout(d) for d in DILATIONS]
    bias = jnp.asarray(np.concatenate([_unit_bias(order) for order, _ in layouts]), F32)
    cur = pl.BlockSpec((None, None, ATT_TILE, LANES), lambda bi, hb, ii: (bi, hb, ii, 0))
    prev = pl.BlockSpec((None, None, ATT_TILE, LANES),
                        lambda bi, hb, ii: (bi, hb, jnp.maximum(ii - 1, 0), 0))
    acc = pltpu.VMEM((ATT_TILE, LANES), F32)
    return pl.pallas_call(
        functools.partial(_prompt_attn_body, layouts),
        grid=(b, N_HEAD_BLOCKS, n_i),
        in_specs=[pl.BlockSpec(memory_space=pltpu.VMEM), cur, cur, prev, cur, prev],
        out_specs=pl.BlockSpec((None, ATT_TILE, LANES), lambda bi, hb, ii: (bi, ii, hb)),
        out_shape=jax.ShapeDtypeStruct((b, t, D_ATT), BF16),
        scratch_shapes=[pltpu.VMEM((2 * PIPE_UNITS * HEADS_PER_BLOCK, BAND, 2 * BAND), BF16),
                        acc] + [acc] * (3 * len(DILATIONS)),
        compiler_params=pltpu.CompilerParams(dimension_semantics=("arbitrary",) * 3,
                                             vmem_limit_bytes=VMEM_LIMIT),
        name="prompt_attn",
    )(bias, qa, ka, ka, va, va)


def _out_ffn(x, ya, yb, wo_ref, gf_ref, wg_ref, wu_ref, wd_ref, cw_ref, cb_ref, gfin_ref, act_scr,
             shifted):
    x1 = (x + jnp.dot(ya, wo_ref[0:D_LRU, :], preferred_element_type=F32)
          + jnp.dot(yb, wo_ref[D_LRU:, :], preferred_element_type=F32))
    h = _rms(x1, gf_ref[...]).astype(BF16)
    for ci in range(D_FF // FF_CHUNK):
        cols = slice(ci * FF_CHUNK, (ci + 1) * FF_CHUNK)
        uf = jnp.dot(h, wg_ref[:, cols], preferred_element_type=F32)
        back1, back2 = shifted(uf, cols)
        uc = (cb_ref[:, cols] + back2 * cw_ref[0:1, cols] + back1 * cw_ref[1:2, cols]
              + uf * cw_ref[2:3, cols])
        up = jnp.dot(h, wu_ref[:, cols], preferred_element_type=F32)
        act_scr[:, cols] = (_gelu(uc) * up).astype(BF16)
    x2 = x1 + jnp.dot(act_scr[...], wd_ref[...], preferred_element_type=F32)
    return _rms(x2, gfin_ref[...])


def _prompt_out_body(x_ref, ya_ref, yb_ref, wo_ref, gf_ref, wg_ref, wu_ref, wd_ref, cw_ref, cb_ref,
                     gfin_ref, y_ref, ftail_ref, tail_scr, act_scr):
    t = pl.program_id(1)

    @pl.when(t == 0)
    def _():
        tail_scr[...] = jnp.zeros_like(tail_scr)

    def shifted(uf, cols):
        tail = tail_scr[:, cols]
        tail_scr[:, cols] = uf[uf.shape[0] - SUBLANES:]
        return _shift_rows(uf, tail, 1), _shift_rows(uf, tail, 2)

    y_ref[...] = _out_ffn(x_ref[...], ya_ref[...], yb_ref[...], wo_ref, gf_ref, wg_ref, wu_ref, wd_ref,
                          cw_ref, cb_ref, gfin_ref, act_scr, shifted)
    ftail_ref[...] = tail_scr[...]


def _prompt_out(x, ya, yb, w_out, g_ffn, w_gate, w_up, w_down, conv_w, conv_b, g_final):
    b, t, _ = x.shape
    whole = pl.BlockSpec(memory_space=pltpu.VMEM)

    def rows(width):
        return pl.BlockSpec((None, ROW_TILE, width), lambda bi, ti: (bi, ti, 0))

    return pl.pallas_call(
        _prompt_out_body,
        grid=(b, t // ROW_TILE),
        in_specs=[rows(D_MODEL), rows(D_LRU), rows(D_ATT)] + [whole] * 8,
        out_specs=[rows(D_MODEL), pl.BlockSpec((None, SUBLANES, D_FF), lambda bi, ti: (bi, 0, 0))],
        out_shape=[jax.ShapeDtypeStruct((b, t, D_MODEL), F32),
                   jax.ShapeDtypeStruct((b, SUBLANES, D_FF), F32)],
        scratch_shapes=[pltpu.VMEM((SUBLANES, D_FF), F32), pltpu.VMEM((ROW_TILE, D_FF), BF16)],
        compiler_params=pltpu.CompilerParams(dimension_semantics=("arbitrary", "arbitrary"),
                                             vmem_limit_bytes=VMEM_LIMIT),
        name="prompt_out",
    )(x, ya, yb, w_out, g_ffn, w_gate, w_up, w_down, conv_w, conv_b, g_final)


def _sample_in_body(n_b, n_s, x_ref, hist_ref, h0_ref, g_ref, win_ref, cw_ref, cb_ref, wgate_ref,
                    bgate_ref, lam_ref, c_ref, s1_ref, s2_ref,
                    ya_ref, q_ref, k_ref, v_ref, cst_ref, hlast_ref):
    z = _bdot(_rms(x_ref[...], g_ref[...]), win_ref[...])
    xa = z[:, :D_LRU]
    ga = z[:, D_LRU:2 * D_LRU]
    q = z[:, 2 * D_LRU:2 * D_LRU + D_ATT]
    k = z[:, 2 * D_LRU + D_ATT:2 * D_LRU + 2 * D_ATT]
    v = z[:, 2 * D_LRU + 2 * D_ATT:]

    ext = jnp.concatenate([hist_ref[...], xa], axis=0)
    xa_c = cb_ref[...] + ext[0:n_s * n_b] * cw_ref[0:1, :]
    for j in range(1, CONV_A):
        xa_c = xa_c + ext[j * n_b:(j + n_s) * n_b] * cw_ref[j:j + 1, :]
    cst_ref[...] = ext[n_s * n_b:]

    a, u = _lru_coeffs(xa_c, wgate_ref, bgate_ref, lam_ref)
    h = h0_ref[...]
    hs = []
    for s in range(n_s):
        h = a[s * n_b:(s + 1) * n_b] * h + u[s * n_b:(s + 1) * n_b]
        hs.append(h)
    hlast_ref[...] = h
    ya_ref[...] = (jnp.concatenate(hs, axis=0) * _gelu(ga)).astype(BF16)

    c, s1, s2 = c_ref[...], s1_ref[...], s2_ref[...]
    q_ref[...] = jnp.concatenate(_rope(q, c, s1, s2), axis=1) * ATTN_SCALE
    k_ref[...] = jnp.concatenate(_rope(k, c, s1, s2), axis=1)
    v_ref[...] = v


def _sample_in(x, hist, h0, g_mix, w_in, conv_w, conv_b, w_gate, b_gate, lam, tabs, n_b, n_s):
    rows = n_b * n_s
    att = jax.ShapeDtypeStruct((rows, D_ATT), F32)
    return pl.pallas_call(
        functools.partial(_sample_in_body, n_b, n_s),
        out_shape=[jax.ShapeDtypeStruct((rows, D_LRU), BF16), att, att, att,
                   jax.ShapeDtypeStruct(((CONV_A - 1) * n_b, D_LRU), F32),
                   jax.ShapeDtypeStruct((n_b, D_LRU), F32)],
        compiler_params=pltpu.CompilerParams(vmem_limit_bytes=VMEM_LIMIT),
        name="sample_in",
    )(x, hist, h0, g_mix, w_in, conv_w, conv_b, w_gate, b_gate, lam, *tabs)


_NT = (((1,), (1,)), ((), ()))


def _sample_scores(heads, q_ref, kn_ref, kt_ref):
    out = []
    for h in heads:
        q = q_ref[h].astype(BF16)
        s = jnp.dot(q, kt_ref[h].astype(BF16), preferred_element_type=F32)
        sn = lax.dot_general(q, kn_ref[h].astype(BF16), _NT, preferred_element_type=F32)
        out.append((s, sn))
    return out


def _sample_probs(scores, mult_ref, multn_ref):
    mult = mult_ref[...]
    multn = multn_ref[...]
    out = []
    for s, sn in scores:
        s = jnp.where(mult > 0.0, s, NEG_INF)
        sn = jnp.where(multn > 0.0, sn, NEG_INF)
        m = jnp.maximum(jnp.max(s, axis=-1, keepdims=True), jnp.max(sn, axis=-1, keepdims=True))
        p = mult * jnp.exp(s - m)
        pn = multn * jnp.exp(sn - m)
        l = jnp.sum(p, axis=-1, keepdims=True) + jnp.sum(pn, axis=-1, keepdims=True)
        out.append((p.astype(BF16), pn.astype(BF16), l))
    return out


def _sample_values(heads, probs, vn_ref, vt_ref, o_ref):
    for h in heads:
        p, pn, l = probs[h]
        o = lax.dot_general(p, vt_ref[h].astype(BF16), _NT, preferred_element_type=F32)
        o = o + jnp.dot(pn, vn_ref[h].astype(BF16), preferred_element_type=F32)
        o_ref[h] = o / l


def _pattern_counts(n_s, win):
    delta = win + jnp.arange(n_s)[:, None] - jnp.arange(win + n_s)[None, :]
    counts = sum(((delta >= 0) & (delta % d == 0) & (delta <= BAND * d)).astype(F32) for d in DILATIONS)
    return counts[:, :win], counts[:, win:]


def _sample_out_body(n_b, x_ref, ya_ref, yb_ref, hist_ref, wo_ref, gf_ref, wg_ref, wu_ref, wd_ref,
                     cw_ref, cb_ref, gfin_ref, y_ref, fst_ref, act_scr):
    n_rows = x_ref.shape[0]

    def shifted(uf, cols):
        ext = jnp.concatenate([hist_ref[:, cols], uf], axis=0)
        fst_ref[:, cols] = ext[n_rows:]
        return ext[n_b:n_b + n_rows], ext[0:n_rows]

    y_ref[...] = _out_ffn(x_ref[...], ya_ref[...], yb_ref[...].astype(BF16), wo_ref, gf_ref, wg_ref,
                          wu_ref, wd_ref, cw_ref, cb_ref, gfin_ref, act_scr, shifted)


def _sample_out(x, ya, yb, hist, w_out, g_ffn, w_gate, w_up, w_down, conv_w, conv_b, g_final, n_b):
    rows = x.shape[0]
    return pl.pallas_call(
        functools.partial(_sample_out_body, n_b),
        out_shape=[jax.ShapeDtypeStruct((rows, D_MODEL), F32),
                   jax.ShapeDtypeStruct(((CONV_F - 1) * n_b, D_FF), F32)],
        scratch_shapes=[pltpu.VMEM((rows, D_FF), BF16)],
        compiler_params=pltpu.CompilerParams(vmem_limit_bytes=VMEM_LIMIT),
        name="sample_out",
    )(x, ya, yb, hist, w_out, g_ffn, w_gate, w_up, w_down, conv_w, conv_b, g_final)


def _rope_tables(pos):
    n = pos.shape[0]
    inv_freq = ROPE_THETA ** (-2.0 * jnp.arange(ROT_HALF, dtype=F32) / ROT_DIM)
    ang = pos[:, None] * inv_freq[None, :]
    cos, sin = jnp.cos(ang), jnp.sin(ang)
    rest = HEAD_DIM - ROT_DIM
    zeros_h = jnp.zeros((n, ROT_HALF), F32)
    c = jnp.concatenate([cos, cos, jnp.ones((n, rest), F32)], axis=1)
    s1 = jnp.concatenate([zeros_h, sin, jnp.zeros((n, rest), F32)], axis=1)
    s2 = jnp.concatenate([-sin, zeros_h, jnp.zeros((n, rest), F32)], axis=1)
    return tuple(jnp.tile(x, (1, HEADS_PER_BLOCK)) for x in (c, s1, s2))


def _block_diag(w):
    nb, n, _ = w.shape
    eye = jnp.eye(nb, dtype=w.dtype)
    return (eye[:, None, :, None] * w[:, :, None, :]).reshape(nb * n, nb * n)


def kernel(x_prompt, x_sample, state_conv_a, state_lru_h, cache_win_k, cache_win_v, state_conv_ffn,
           g_mix, w_in, conv_a_w, conv_a_b, lru_w_r, lru_b_r, lru_w_i, lru_b_i, lru_lambda, w_out,
           g_ffn, w_ffn_gate, conv_f_w, conv_f_b, w_ffn_up, w_ffn_down, g_final):
    n_p, seq_p, _ = x_prompt.shape
    n_s, seq_s, _ = x_sample.shape
    depth = w_in.shape[0]
    win = cache_win_k.shape[2]
    assert depth == 1
    assert seq_p % ATT_TILE == 0 and win == MAX_WINDOW and 2 * seq_s == SUBLANES

    tabs_p = _rope_tables(jnp.arange(seq_p, dtype=F32))
    tabs_s = tuple(jnp.repeat(x, n_s, axis=0)
                   for x in _rope_tables(PAST_LEN + jnp.arange(seq_s, dtype=F32)))
    g_fin = g_final.reshape(1, D_MODEL)

    xp = x_prompt
    xs = x_sample.transpose(1, 0, 2).reshape(seq_s * n_s, D_MODEL)
    outs = [[] for _ in range(10)]
    for layer in range(depth):
        w_in_b = w_in[layer].astype(BF16)
        w_gate = jnp.concatenate([_block_diag(lru_w_r[layer]), _block_diag(lru_w_i[layer])],
                                 axis=1).astype(BF16)
        b_gate = jnp.concatenate([lru_b_r[layer], lru_b_i[layer]]).reshape(1, 2 * D_LRU)
        lam = lru_lambda[layer].reshape(1, D_LRU)
        g_mix_l = g_mix[layer].reshape(1, D_MODEL)
        g_ffn_l = g_ffn[layer].reshape(1, D_MODEL)
        conv_a_b_l = conv_a_b[layer].reshape(1, D_LRU)
        conv_f_b_l = conv_f_b[layer].reshape(1, D_FF)
        w_out_b = w_out[layer].astype(BF16)
        w_fg = w_ffn_gate[layer].astype(BF16)
        w_fu = w_ffn_up[layer].astype(BF16)
        w_fd = w_ffn_down[layer].astype(BF16)

        hist_a = state_conv_a[layer].transpose(1, 0, 2).reshape((CONV_A - 1) * n_s, D_LRU)
        hist_f = state_conv_ffn[layer].transpose(1, 0, 2).reshape((CONV_F - 1) * n_s, D_FF)
        ya_s, q_s, k_s, v_s, cst_s, hlast_s = _sample_in(
            xs, hist_a, state_lru_h[layer], g_mix_l, w_in_b, conv_a_w[layer], conv_a_b_l, w_gate,
            b_gate, lam, tabs_s, n_s, seq_s)

        def per_head(x):
            return x.reshape(seq_s, n_s, N_HEADS, HEAD_DIM).transpose(1, 2, 0, 3)

        ya, qa, ka, va, kn, vn, ctail, hlast, yb_s = _prompt_in(
            xp, g_mix_l, w_in_b, conv_a_w[layer], conv_a_b_l, w_gate, b_gate, lam, tabs_p,
            per_head(q_s), per_head(k_s), per_head(v_s),
            cache_win_k[layer].transpose(0, 2, 3, 1), cache_win_v[layer].transpose(0, 2, 3, 1))
        yb = _prompt_attn(qa, ka, va)
        yp, ftail = _prompt_out(xp, ya, yb, w_out_b, g_ffn_l, w_fg, w_fu, w_fd, conv_f_w[layer],
                                conv_f_b_l, g_fin)
        outs[0].append(ctail[:, SUBLANES - (CONV_A - 1):])
        outs[1].append(hlast[:, 0])
        outs[2].append(kn.reshape(n_p, MAX_WINDOW, N_HEADS, HEAD_DIM))
        outs[3].append(vn.reshape(n_p, MAX_WINDOW, N_HEADS, HEAD_DIM))
        outs[4].append(ftail[:, SUBLANES - (CONV_F - 1):])

        yb_s = yb_s.transpose(2, 0, 1, 3).reshape(seq_s * n_s, D_ATT)
        ys, fst_s = _sample_out(xs, ya_s, yb_s, hist_f, w_out_b, g_ffn_l, w_fg, w_fu, w_fd,
                                conv_f_w[layer], conv_f_b_l, g_fin, n_s)
        outs[5].append(cst_s.reshape(CONV_A - 1, n_s, D_LRU).transpose(1, 0, 2))
        outs[6].append(hlast_s)
        outs[7].append(k_s.reshape(seq_s, n_s, N_HEADS, HEAD_DIM).transpose(1, 0, 2, 3))
        outs[8].append(v_s.reshape(seq_s, n_s, N_HEADS, HEAD_DIM).transpose(1, 0, 2, 3))
        outs[9].append(fst_s.reshape(CONV_F - 1, n_s, D_FF).transpose(1, 0, 2))
        xp, xs = yp, ys

    y_sample = xs.reshape(seq_s, n_s, D_MODEL).transpose(1, 0, 2)
    return (xp, y_sample) + tuple(jnp.stack(o, axis=0) for o in outs)
```
